```python
import jax, jax.numpy as jnp
from jax import lax
import numpy as np

D_MODEL = 2048
BATCH = 1
SEQ = 8192
DEPTH = 1
DEC_BATCH = 128
DEC_SEQ = 8
PAST_LEN = 2048
PAGE_SIZE = 128

GLA_HEADS = 4
GLA_DK = D_MODEL // (2 * GLA_HEADS)
GLA_DV = D_MODEL // GLA_HEADS
GLA_RANK = 16
GLA_TAU = 16.0
GLA_CHUNK = 64
DSA_HEADS = 16
DSA_HD = D_MODEL // DSA_HEADS
IDX_HEADS = 16
IDX_HD = 128
TOPK_MAX = 256
Q_BLOCK = 128
D_FF = ((8 * D_MODEL // 3 + 127) // 128) * 128
CONV_W = 3
EPS = 1e-6
MIX_SPLITS = (GLA_HEADS * GLA_DK, GLA_HEADS * GLA_DK, GLA_HEADS * GLA_DV, GLA_HEADS * GLA_DV, GLA_RANK,
              DSA_HEADS * DSA_HD, DSA_HEADS * DSA_HD, DSA_HEADS * DSA_HD,
              IDX_HEADS * IDX_HD, IDX_HEADS, IDX_HD, 2 * D_MODEL)

kernel_name = "hybrid_gla_dsa_convffn_step"


def rms_norm(x, g):
    xf = x.astype(jnp.float32)
    y = xf * lax.rsqrt(jnp.mean(xf * xf, axis=-1, keepdims=True) + EPS)
    return (y * g.astype(jnp.float32)).astype(x.dtype)


def split_mixer_inputs(xn, w_in, w_a2, b_a):
    B, L = xn.shape[:2]
    z = xn @ w_in
    bounds = [int(b) for b in np.cumsum(MIX_SPLITS)[:-1]]
    q_g, k_g, v_g, r_g, a_lr, q_d, k_d, v_d, iq, iw, ik, gates = jnp.split(z, bounds, axis=-1)
    log_a = jax.nn.log_sigmoid((a_lr @ w_a2 + b_a).astype(jnp.float32)) / GLA_TAU
    gla = (q_g.reshape(B, L, GLA_HEADS, GLA_DK) * GLA_DK ** -0.5,
           k_g.reshape(B, L, GLA_HEADS, GLA_DK),
           v_g.reshape(B, L, GLA_HEADS, GLA_DV),
           log_a.reshape(B, L, GLA_HEADS, GLA_DK))
    dsa = (q_d.reshape(B, L, DSA_HEADS, DSA_HD),
           k_d.reshape(B, L, DSA_HEADS, DSA_HD),
           v_d.reshape(B, L, DSA_HEADS, DSA_HD),
           iq.reshape(B, L, IDX_HEADS, IDX_HD),
           iw * IDX_HEADS ** -0.5,
           ik)
    return gla, r_g, dsa, gates


def gla_chunk(S, q, k, v, log_a):
    qf, kf, vf = q.astype(jnp.float32), k.astype(jnp.float32), v.astype(jnp.float32)
    b = jnp.cumsum(log_a.astype(jnp.float32), axis=1)
    C = q.shape[1]
    causal = jnp.tril(jnp.ones((C, C), dtype=bool))[None, :, :, None, None]
    decay = jnp.exp(jnp.where(causal, b[:, :, None] - b[:, None, :], -jnp.inf))
    A = jnp.einsum('bthd,bshd,btshd->bhts', qf, kf, decay)
    o = jnp.einsum('bhts,bshv->bthv', A, vf) + jnp.einsum('bthd,bhdv->bthv', qf * jnp.exp(b), S)
    b_last = b[:, -1]
    k_dec = kf * jnp.exp(b_last[:, None] - b)
    S_new = jnp.exp(b_last)[..., None] * S + jnp.einsum('bshd,bshv->bhdv', k_dec, vf)
    return S_new, o


def gla_scan(S0, q, k, v, log_a):
    B, L = q.shape[:2]
    c = min(GLA_CHUNK, L)
    n = L // c

    def to_chunks(t):
        return jnp.moveaxis(t.reshape(B, n, c, *t.shape[2:]), 1, 0)

    def step(S, xs):
        return gla_chunk(S, *xs)

    S, o = lax.scan(step, S0, (to_chunks(q), to_chunks(k), to_chunks(v), to_chunks(log_a)))
    return S, jnp.moveaxis(o, 0, 1).reshape(B, L, GLA_HEADS, GLA_DV)


def index_scores(iq, iw, ik):
    s = jnp.einsum('bqhd,bsd->bqhs', iq.astype(jnp.float32), ik.astype(jnp.float32)) * IDX_HD ** -0.5
    return jnp.einsum('bqh,bqhs->bqs', iw.astype(jnp.float32), jax.nn.relu(s))


def sparse_attend(q, ks, vs, valid):
    logits = jnp.einsum('bqhd,bqkhd->bqhk', q.astype(jnp.float32), ks.astype(jnp.float32)) * DSA_HD ** -0.5
    logits = jnp.where(valid[:, :, None, :], logits, -jnp.inf)
    p = jax.nn.softmax(logits, axis=-1)
    return jnp.einsum('bqhk,bqkhd->bqhd', p, vs.astype(jnp.float32))


def dsa_prompt(q, k, v, iq, iw, ik):
    B, L = q.shape[:2]
    topk = min(TOPK_MAX, L // 4)
    qb = min(Q_BLOCK, L)
    nb = L // qb
    key_pos = jnp.arange(L)

    def blockify(t):
        return jnp.moveaxis(t.reshape(B, nb, qb, *t.shape[2:]), 1, 0)

    def block(xs):
        q_blk, iq_blk, iw_blk, qpos = xs
        score = index_scores(iq_blk, iw_blk, ik)
        valid = jnp.broadcast_to(key_pos[None, None, :] <= qpos[None, :, None], score.shape)
        _, sel = lax.top_k(jnp.where(valid, score, -jnp.inf), topk)
        sel_valid = jnp.take_along_axis(valid, sel, axis=-1)
        ks = jax.vmap(lambda kk, ii: kk[ii])(k, sel)
        vs = jax.vmap(lambda vv, ii: vv[ii])(v, sel)
        return sparse_attend(q_blk, ks, vs, sel_valid)

    o = lax.map(block, (blockify(q), blockify(iq), blockify(iw), jnp.arange(L).reshape(nb, qb)))
    return jnp.moveaxis(o, 0, 1).reshape(B, L, DSA_HEADS, DSA_HD)


def dsa_sample(q, k_new, v_new, iq, iw, ik_new, cache_k, cache_v, cache_idx_k, page_table):
    DB, T = q.shape[:2]
    past = page_table.shape[1] * PAGE_SIZE
    L = past + T
    topk = min(TOPK_MAX, L // 4)
    ik_past = cache_idx_k[page_table].reshape(DB, past, IDX_HD)
    ik_all = jnp.concatenate([ik_past.astype(ik_new.dtype), ik_new], axis=1)
    score = index_scores(iq, iw, ik_all)
    qpos = past + jnp.arange(T)
    valid = jnp.broadcast_to((jnp.arange(L)[None, :] <= qpos[:, None])[None], score.shape)
    _, sel = lax.top_k(jnp.where(valid, score, -jnp.inf), topk)
    sel_valid = jnp.take_along_axis(valid, sel, axis=-1)

    def per_seq(xs):
        q_s, sel_s, valid_s, pt, kn, vn = xs
        in_past = (sel_s < past)[..., None, None]
        page = pt[jnp.clip(sel_s, 0, past - 1) // PAGE_SIZE]
        off = sel_s % PAGE_SIZE
        new_i = jnp.clip(sel_s - past, 0, T - 1)
        ks = jnp.where(in_past, cache_k[page, off], kn[new_i])
        vs = jnp.where(in_past, cache_v[page, off], vn[new_i])
        return sparse_attend(q_s[None], ks[None], vs[None], valid_s[None])[0]

    return lax.map(per_seq, (q, sel, sel_valid, page_table, k_new, v_new))


def merge_branches(o_gla, r_g, o_dsa, gates, gla_norm, w_gla_proj, w_dsa_proj, w_out):
    B, L = o_gla.shape[:2]
    o = o_gla.astype(jnp.float32)
    o = o * lax.rsqrt(jnp.mean(o * o, axis=-1, keepdims=True) + EPS) * gla_norm.astype(jnp.float32).reshape(GLA_HEADS, GLA_DV)
    o = o.reshape(B, L, GLA_HEADS * GLA_DV).astype(r_g.dtype) * jax.nn.silu(r_g)
    branch_a = o @ w_gla_proj
    branch_b = o_dsa.reshape(B, L, DSA_HEADS * DSA_HD).astype(r_g.dtype) @ w_dsa_proj
    g_a, g_b = jnp.split(gates, 2, axis=-1)
    return (jax.nn.sigmoid(g_a) * branch_a + jax.nn.sigmoid(g_b) * branch_b) @ w_out


def conv_ffn(xn, conv_prev, w_up, w_conv, b_conv, w_down):
    u = xn @ w_up
    L = u.shape[1]
    u_pad = jnp.concatenate([conv_prev.astype(u.dtype), u], axis=1)
    c = sum(u_pad[:, j:j + L] * w_conv[j] for j in range(CONV_W)) + b_conv
    a, b = jnp.split(c, 2, axis=-1)
    return (jax.nn.gelu(a) * b) @ w_down, u_pad[:, -(CONV_W - 1):]


def prompt_layer(x, norm_mix, w_in, w_a2, b_a, gla_norm, w_gla_proj, w_dsa_proj, w_out,
                 norm_ffn, w_up, w_conv, b_conv, w_down):
    B = x.shape[0]
    xn = rms_norm(x, norm_mix)
    (q_g, k_g, v_g, la), r_g, (q_d, k_d, v_d, iq, iw, ik), gates = split_mixer_inputs(xn, w_in, w_a2, b_a)
    S0 = jnp.zeros((B, GLA_HEADS, GLA_DK, GLA_DV), jnp.float32)
    S, o_gla = gla_scan(S0, q_g, k_g, v_g, la)
    o_dsa = dsa_prompt(q_d, k_d, v_d, iq, iw, ik)
    h = x + merge_branches(o_gla, r_g, o_dsa, gates, gla_norm, w_gla_proj, w_dsa_proj, w_out)
    conv0 = jnp.zeros((B, CONV_W - 1, 2 * D_FF), x.dtype)
    f, conv_state = conv_ffn(rms_norm(h, norm_ffn), conv0, w_up, w_conv, b_conv, w_down)
    return h + f, (k_d, v_d, ik, S.astype(x.dtype), conv_state)


def sample_layer(x, cache_k, cache_v, cache_idx_k, state_gla, conv_prev, page_table,
                 norm_mix, w_in, w_a2, b_a, gla_norm, w_gla_proj, w_dsa_proj, w_out,
                 norm_ffn, w_up, w_conv, b_conv, w_down):
    xn = rms_norm(x, norm_mix)
    (q_g, k_g, v_g, la), r_g, (q_d, k_d, v_d, iq, iw, ik), gates = split_mixer_inputs(xn, w_in, w_a2, b_a)
    S, o_gla = gla_chunk(state_gla.astype(jnp.float32), q_g, k_g, v_g, la)
    o_dsa = dsa_sample(q_d, k_d, v_d, iq, iw, ik, cache_k, cache_v, cache_idx_k, page_table)
    h = x + merge_branches(o_gla, r_g, o_dsa, gates, gla_norm, w_gla_proj, w_dsa_proj, w_out)
    f, conv_state = conv_ffn(rms_norm(h, norm_ffn), conv_prev, w_up, w_conv, b_conv, w_down)
    return h + f, (k_d, v_d, ik, S.astype(state_gla.dtype), conv_state)


def setup_inputs(seed: int = 0) -> dict:
    key = jax.random.key(seed)
    ks = jax.random.split(key, 24)
    f32 = jnp.float32

    def nrm(k, shape, s=1.0):
        return jax.random.normal(k, shape, f32) * s

    n_pages = PAST_LEN // PAGE_SIZE
    n_used = DEC_BATCH * n_pages
    n_pool = n_used + (n_used + 3) // 4
    n_mix = sum(MIX_SPLITS)
    hk, hv = GLA_HEADS * GLA_DK, GLA_HEADS * GLA_DV
    hd = DSA_HEADS * DSA_HD
    return {
        "x_prompt": nrm(ks[0], (BATCH, SEQ, D_MODEL)),
        "x_sample": nrm(ks[1], (DEC_BATCH, DEC_SEQ, D_MODEL)),
        "cache_k": nrm(ks[2], (DEPTH, n_pool, PAGE_SIZE, DSA_HEADS, DSA_HD)),
        "cache_v": nrm(ks[3], (DEPTH, n_pool, PAGE_SIZE, DSA_HEADS, DSA_HD)),
        "cache_idx_k": nrm(ks[4], (DEPTH, n_pool, PAGE_SIZE, IDX_HD)),
        "state_gla": nrm(ks[5], (DEPTH, DEC_BATCH, GLA_HEADS, GLA_DK, GLA_DV), 0.5),
        "state_ffn_conv": nrm(ks[6], (DEPTH, DEC_BATCH, CONV_W - 1, 2 * D_FF)),
        "page_table": jax.random.permutation(ks[7], n_pool)[:n_used].reshape(DEC_BATCH, n_pages).astype(jnp.int32),
        "norm_mix": 1.0 + nrm(ks[8], (DEPTH, D_MODEL), 0.02),
        "w_in": nrm(ks[9], (DEPTH, D_MODEL, n_mix), D_MODEL ** -0.5),
        "w_a2": nrm(ks[10], (DEPTH, GLA_RANK, hk), GLA_RANK ** -0.5),
        "b_a": nrm(ks[11], (DEPTH, hk), 0.1),
        "gla_norm": 1.0 + nrm(ks[12], (DEPTH, hv), 0.02),
        "w_gla_proj": nrm(ks[13], (DEPTH, hv, D_MODEL), hv ** -0.5),
        "w_dsa_proj": nrm(ks[14], (DEPTH, hd, D_MODEL), hd ** -0.5),
        "w_out": nrm(ks[15], (DEPTH, D_MODEL, D_MODEL), D_MODEL ** -0.5),
        "norm_ffn": 1.0 + nrm(ks[16], (DEPTH, D_MODEL), 0.02),
        "w_up": nrm(ks[17], (DEPTH, D_MODEL, 2 * D_FF), D_MODEL ** -0.5),
        "w_conv": nrm(ks[18], (DEPTH, CONV_W, 2 * D_FF), CONV_W ** -0.5),
        "b_conv": nrm(ks[19], (DEPTH, 2 * D_FF), 0.02),
        "w_down": nrm(ks[20], (DEPTH, D_FF, D_MODEL), D_FF ** -0.5),
        "norm_final": 1.0 + nrm(ks[21], (D_MODEL,), 0.02),
    }


def reference(x_prompt, x_sample, cache_k, cache_v, cache_idx_k, state_gla, state_ffn_conv, page_table,
              norm_mix, w_in, w_a2, b_a, gla_norm, w_gla_proj, w_dsa_proj, w_out,
              norm_ffn, w_up, w_conv, b_conv, w_down, norm_final):
    yp, ys = x_prompt, x_sample
    new_p, new_s = [], []
    for l in range(DEPTH):
        lw = (norm_mix[l], w_in[l], w_a2[l], b_a[l], gla_norm[l], w_gla_proj[l], w_dsa_proj[l], w_out[l],
              norm_ffn[l], w_up[l], w_conv[l], b_conv[l], w_down[l])
        yp, st_p = prompt_layer(yp, *lw)
        ys, st_s = sample_layer(ys, cache_k[l], cache_v[l], cache_idx_k[l], state_gla[l], state_ffn_conv[l],
                                page_table, *lw)
        new_p.append(st_p)
        new_s.append(st_s)

    def stacked(states, i):
        return jnp.stack([s[i] for s in states])

    y_prompt = rms_norm(yp, norm_final)
    y_sample = rms_norm(ys, norm_final)
    return (y_prompt, y_sample,
            stacked(new_p, 0), stacked(new_p, 1), stacked(new_p, 2), stacked(new_p, 3), stacked(new_p, 4),
            stacked(new_s, 0), stacked(new_s, 1), stacked(new_s, 2), stacked(new_s, 3), stacked(new_s, 4))
```

```python
import functools

import jax
import jax.numpy as jnp
import numpy as np
from jax import lax
from jax.experimental import pallas as pl
from jax.experimental.pallas import tpu as pltpu

_F32 = jnp.float32
_MXU = jnp.bfloat16
_EPS = 1e-6
_GLA_TAU = 16.0
_GLA_SUB = 16
_GLA_PAD = 128
_TOPK_MAX = 256
_LANES = 128
_NEG = -1e30
_INT_MIN = -(2 ** 31)
_VMEM_LIMIT = 56 * 1024 * 1024


def _cparams(sem):
    return pltpu.CompilerParams(dimension_semantics=sem, vmem_limit_bytes=_VMEM_LIMIT)


def _pick(n, cands):
    for c in cands:
        if n % c == 0:
            return c
    raise ValueError(f"no tile in {cands} divides {n}")


def _dot(a, b):
    return jnp.dot(a, b, preferred_element_type=_F32)


def _dot_nt(a, b):
    return lax.dot_general(a, b, (((1,), (1,)), ((), ())), preferred_element_type=_F32)


def _rms(x, g):
    return x * lax.rsqrt(jnp.mean(x * x, axis=-1, keepdims=True) + _EPS) * g


def _sort_key(x):
    bits = lax.bitcast_convert_type(x, jnp.int32)
    return bits ^ ((bits >> 31) & jnp.int32(0x7FFFFFFF))


def _kth_largest_key(count_ge, shape, k):
    kf = float(k)
    lo = jnp.where(count_ge(jnp.zeros(shape, jnp.int32)) >= kf, 0, _INT_MIN).astype(jnp.int32)

    def step(it, lo):
        cand = lo + (jnp.int32(1) << (30 - it))
        return jnp.where(count_ge(cand) >= kf, cand, lo)

    return lax.fori_loop(0, 31, step, lo)


def _norm_matmul_kernel(x_ref, g_ref, w_ref, o_ref, ob_ref, xn_sc):
    @pl.when(pl.program_id(1) == 0)
    def _():
        xn_sc[...] = _rms(x_ref[...], g_ref[...]).astype(xn_sc.dtype)

    acc = _dot(xn_sc[...], w_ref[...])
    o_ref[...] = acc
    ob_ref[...] = acc.astype(ob_ref.dtype)


def _norm_matmul(x, gain, w):
    M, K = x.shape
    N = w.shape[1]
    tm = _pick(M, (512, 256, 128))
    tn = _pick(N, (512, 256, 128))
    return pl.pallas_call(
        _norm_matmul_kernel,
        grid=(M // tm, N // tn),
        in_specs=[pl.BlockSpec((tm, K), lambda i, j: (i, 0)),
                  pl.BlockSpec((1, K), lambda i, j: (0, 0)),
                  pl.BlockSpec((K, tn), lambda i, j: (0, j))],
        out_specs=[pl.BlockSpec((tm, tn), lambda i, j: (i, j)),
                   pl.BlockSpec((tm, tn), lambda i, j: (i, j))],
        out_shape=[jax.ShapeDtypeStruct((M, N), _F32), jax.ShapeDtypeStruct((M, N), _MXU)],
        scratch_shapes=[pltpu.VMEM((tm, K), _MXU)],
        compiler_params=_cparams(("parallel", "arbitrary")),
        name="norm_in_proj",
    )(x, gain, w)


def _gla_kernel(*refs, C, has_init, dk, rank_lanes):
    if has_init:
        q_ref, k_ref, v_ref, r_ref, a_ref, wa_ref, ba_ref, gn_ref, s0_ref, o_ref, so_ref, S_sc, b_sc = refs
    else:
        q_ref, k_ref, v_ref, r_ref, a_ref, wa_ref, ba_ref, gn_ref, o_ref, so_ref, S_sc, b_sc = refs
    P = _GLA_PAD
    c = pl.program_id(2)

    @pl.when(c == 0)
    def _():
        if has_init:
            S_sc[...] = s0_ref[0, 0]
        else:
            S_sc[...] = jnp.zeros_like(S_sc)

    def padded(x):
        if C == P:
            return x
        return jnp.concatenate([x, jnp.zeros((P - C, x.shape[1]), x.dtype)], axis=0)

    q = padded(q_ref[...]) * (dk ** -0.5)
    k = padded(k_ref[...])
    v = padded(v_ref[...]).astype(_MXU)
    zz = _dot(padded(a_ref[...]).astype(_MXU), wa_ref[...]) + ba_ref[...]
    la = (jnp.minimum(zz, 0.0) - jnp.log(1.0 + jnp.exp(-jnp.abs(zz)))) / _GLA_TAU
    row = lax.broadcasted_iota(jnp.int32, la.shape, 0)
    if C < P:
        la = jnp.where(row < C, la, 0.0)
    b = la
    sh = 1
    while sh < P:
        b = b + jnp.where(row >= sh, pltpu.roll(b, sh, axis=0), 0.0)
        sh *= 2
    b_sc[...] = b

    S = S_sc[...]
    o = _dot((q * jnp.exp(b)).astype(_MXU), S.astype(_MXU))
    n_sub = -(-C // _GLA_SUB)
    krow = lax.broadcasted_iota(jnp.int32, (P, dk), 0)
    a_rows = []
    for i in range(n_sub):
        lo, hi = i * _GLA_SUB, (i + 1) * _GLA_SUB
        anchor = b_sc[pl.ds(lo - 1, 1), :] if i > 0 else jnp.zeros((1, dk), _F32)
        qt = (q[lo:hi] * jnp.exp(b[lo:hi] - anchor)).astype(_MXU)
        kt = (k * jnp.exp(jnp.where(krow < hi, anchor - b, 0.0))).astype(_MXU)
        a_i = _dot_nt(qt, kt)
        r_i = lax.broadcasted_iota(jnp.int32, a_i.shape, 0) + lo
        c_i = lax.broadcasted_iota(jnp.int32, a_i.shape, 1)
        a_rows.append(jnp.where(c_i <= r_i, a_i, 0.0))
    if n_sub * _GLA_SUB < P:
        a_rows.append(jnp.zeros((P - n_sub * _GLA_SUB, P), _F32))
    A = jnp.concatenate(a_rows, axis=0)
    o = o + _dot(A.astype(_MXU), v)
    bT = b.T
    b_last = bT[:, P - 1:P]
    kdT = (k.T * jnp.exp(b_last - bT)).astype(_MXU)
    S_new = jnp.exp(b_last) * S + _dot(kdT, v)
    S_sc[...] = S_new
    so_ref[0, 0] = S_new

    o = o[:C]
    o = _rms(o, gn_ref[...])
    r = r_ref[...]
    o_ref[...] = o * (r * jax.nn.sigmoid(r))


def _gla(z, wa, ba, gn, s0, *, row0, n_seq, n_chunks, C, lay):
    H, dk, dv = lay["gla_heads"], lay["gla_dk"], lay["gla_dv"]
    rb0 = row0 // C

    def rows(s, c):
        return rb0 + s * n_chunks + c

    in_specs = [
        pl.BlockSpec((C, dk), lambda s, h, c: (rows(s, c), lay["qg"] // dk + h)),
        pl.BlockSpec((C, dk), lambda s, h, c: (rows(s, c), lay["kg"] // dk + h)),
        pl.BlockSpec((C, dv), lambda s, h, c: (rows(s, c), lay["vg"] // dv + h)),
        pl.BlockSpec((C, dv), lambda s, h, c: (rows(s, c), lay["rg"] // dv + h)),
        pl.BlockSpec((C, _LANES), lambda s, h, c: (rows(s, c), lay["small"] // _LANES)),
        pl.BlockSpec((_LANES, dk), lambda s, h, c: (0, h)),
        pl.BlockSpec((1, dk), lambda s, h, c: (0, h)),
        pl.BlockSpec((1, dv), lambda s, h, c: (0, h)),
    ]
    args = [z, z, z, z, z, wa, ba, gn]
    if s0 is not None:
        in_specs.append(pl.BlockSpec((1, 1, dk, dv), lambda s, h, c: (s, h, 0, 0)))
        args.append(s0)
    return pl.pallas_call(
        functools.partial(_gla_kernel, C=C, has_init=s0 is not None, dk=dk, rank_lanes=_LANES),
        grid=(n_seq, H, n_chunks),
        in_specs=in_specs,
        out_specs=[pl.BlockSpec((C, dv), lambda s, h, c: (s * n_chunks + c, h)),
                   pl.BlockSpec((1, 1, dk, dv), lambda s, h, c: (s, h, 0, 0))],
        out_shape=[jax.ShapeDtypeStruct((n_seq * n_chunks * C, H * dv), _F32),
                   jax.ShapeDtypeStruct((n_seq, H, dk, dv), _F32)],
        scratch_shapes=[pltpu.VMEM((dk, dv), _F32), pltpu.VMEM((_GLA_PAD, dk), _F32)],
        compiler_params=_cparams(("parallel", "parallel", "arbitrary")),
        name="gla",
    )(*args)


def _dsa_index_kernel(iq_ref, iw_ref, ik_ref, mask_ref, keys_sc, *, tq, tk, topk, n_heads, hd, iw_lane0, n_ktiles):
    qi = pl.program_id(0)
    nk = (qi + 1) * (tq // tk)
    w = iw_ref[:, iw_lane0:iw_lane0 + n_heads] * (n_heads ** -0.5)
    rows = qi * tq + lax.broadcasted_iota(jnp.int32, (tq, tk), 0)
    cols = lax.broadcasted_iota(jnp.int32, (tq, tk), 1)

    def score_tile(kj, carry):
        off = pl.multiple_of(kj * tk, tk)
        ik = ik_ref[pl.ds(off, tk), :]
        acc = jnp.zeros((tq, tk), _F32)
        for h in range(n_heads):
            s = _dot_nt(iq_ref[:, h * hd:(h + 1) * hd], ik) * (hd ** -0.5)
            acc = acc + jnp.maximum(s, 0.0) * w[:, h:h + 1]
        keys_sc[:, pl.ds(off, tk)] = jnp.where(cols + off <= rows, _sort_key(acc), _INT_MIN)
        return carry

    lax.fori_loop(0, nk, score_tile, 0)

    def count_ge(cand):
        def body(kj, acc):
            off = pl.multiple_of(kj * tk, tk)
            return acc + jnp.where(keys_sc[:, pl.ds(off, tk)] >= cand, 1.0, 0.0)

        acc = lax.fori_loop(0, nk, body, jnp.zeros((tq, tk), _F32))
        return jnp.sum(acc, axis=1, keepdims=True)

    thr = jnp.maximum(_kth_largest_key(count_ge, (tq, 1), topk), _INT_MIN + 1)

    def write_mask(kj, carry):
        off = pl.multiple_of(kj * tk, tk)
        mask_ref[:, pl.ds(off, tk)] = jnp.where(keys_sc[:, pl.ds(off, tk)] >= thr, 1.0, 0.0).astype(mask_ref.dtype)
        return carry

    lax.fori_loop(0, nk, write_mask, 0)

    def write_zero(kj, carry):
        off = pl.multiple_of(kj * tk, tk)
        mask_ref[:, pl.ds(off, tk)] = jnp.zeros((tq, tk), mask_ref.dtype)
        return carry

    lax.fori_loop(nk, n_ktiles, write_zero, 0)


def _dsa_index(z, zb, *, T, topk, lay):
    n_heads, hd = lay["idx_heads"], lay["idx_hd"]
    tq = tk = 128
    width = n_heads * hd
    return pl.pallas_call(
        functools.partial(_dsa_index_kernel, tq=tq, tk=tk, topk=topk, n_heads=n_heads, hd=hd,
                          iw_lane0=lay["iw_lane0"], n_ktiles=T // tk),
        grid=(T // tq,),
        in_specs=[pl.BlockSpec((tq, width), lambda i: (i, lay["iq"] // width)),
                  pl.BlockSpec((tq, _LANES), lambda i: (i, lay["small"] // _LANES)),
                  pl.BlockSpec((T, hd), lambda i: (0, lay["ik"] // hd))],
        out_specs=pl.BlockSpec((tq, T), lambda i: (i, 0)),
        out_shape=jax.ShapeDtypeStruct((T, T), jnp.bfloat16),
        scratch_shapes=[pltpu.VMEM((tq, T), jnp.int32)],
        compiler_params=_cparams(("parallel",)),
        name="dsa_index_topk",
    )(zb, z, zb)


def _dsa_attn_kernel(qi_tab, kj_tab, q_ref, k_ref, v_ref, mask_ref, o_ref, m_sc, l_sc, acc_sc, *, n_heads, hd):
    p = pl.program_id(0)
    qi = qi_tab[p]
    kj = kj_tab[p]

    @pl.when(kj == 0)
    def _():
        m_sc[...] = jnp.full_like(m_sc, _NEG)
        l_sc[...] = jnp.zeros_like(l_sc)
        acc_sc[...] = jnp.zeros_like(acc_sc)

    mk = mask_ref[...].astype(_F32) > 0.5
    for h in range(n_heads):
        sl = slice(h * hd, (h + 1) * hd)
        s = _dot_nt(q_ref[:, sl], k_ref[:, sl]) * (hd ** -0.5)
        s = jnp.where(mk, s, _NEG)
        m_prev = m_sc[h]
        m_new = jnp.maximum(m_prev, jnp.max(s, axis=1, keepdims=True))
        alpha = jnp.exp(m_prev - m_new)
        pr = jnp.exp(s - m_new)
        l_sc[h] = alpha * l_sc[h] + jnp.sum(pr, axis=1, keepdims=True)
        acc_sc[:, sl] = alpha * acc_sc[:, sl] + _dot(pr.astype(_MXU), v_ref[:, sl])
        m_sc[h] = m_new

    @pl.when(kj == qi)
    def _():
        for h in range(n_heads):
            sl = slice(h * hd, (h + 1) * hd)
            o_ref[:, sl] = acc_sc[:, sl] / l_sc[h]


def _dsa_attn(zb, mask, *, T, lay):
    n_heads, hd = lay["dsa_heads"], lay["dsa_hd"]
    width = n_heads * hd
    t = _pick(T, (256, 128))
    nq = T // t
    qi_tab = np.concatenate([np.full(i + 1, i, np.int32) for i in range(nq)])
    kj_tab = np.concatenate([np.arange(i + 1, dtype=np.int32) for i in range(nq)])
    grid_spec = pltpu.PrefetchScalarGridSpec(
        num_scalar_prefetch=2,
        grid=(len(qi_tab),),
        in_specs=[pl.BlockSpec((t, width), lambda p, qt, kt: (qt[p], lay["qd"] // width)),
                  pl.BlockSpec((t, width), lambda p, qt, kt: (kt[p], lay["kd"] // width)),
                  pl.BlockSpec((t, width), lambda p, qt, kt: (kt[p], lay["vd"] // width)),
                  pl.BlockSpec((t, t), lambda p, qt, kt: (qt[p], kt[p]))],
        out_specs=pl.BlockSpec((t, width), lambda p, qt, kt: (qt[p], 0)),
        scratch_shapes=[pltpu.VMEM((n_heads, t, 1), _F32), pltpu.VMEM((n_heads, t, 1), _F32),
                        pltpu.VMEM((t, width), _F32)],
    )
    return pl.pallas_call(
        functools.partial(_dsa_attn_kernel, n_heads=n_heads, hd=hd),
        grid_spec=grid_spec,
        out_shape=jax.ShapeDtypeStruct((T, width), _F32),
        compiler_params=_cparams(("arbitrary",)),
        name="dsa_attn",
    )(jnp.asarray(qi_tab), jnp.asarray(kj_tab), zb, zb, zb, mask)


def _smp_index_kernel(pt_ref, iq_ref, iw_ref, ikn_ref, ikp_ref, mask_ref, lhs_sc, w_sc, keys_sc, *,
                      n_pages, page, Ts, topk, n_heads, hd, iw_lane0):
    p = pl.program_id(1)

    @pl.when(p == 0)
    def _():
        for h in range(n_heads):
            lhs_sc[h * Ts:(h + 1) * Ts, :] = iq_ref[:, h * hd:(h + 1) * hd]
            wcol = iw_ref[:, iw_lane0 + h:iw_lane0 + h + 1] * (n_heads ** -0.5)
            w_sc[h * Ts:(h + 1) * Ts, :] = jnp.broadcast_to(wcol, (Ts, page))

    lhs = lhs_sc[...].astype(_MXU)

    def scores(ik):
        s = _dot_nt(lhs, ik) * (hd ** -0.5)
        r = jnp.maximum(s, 0.0) * w_sc[...]
        out = r[0:Ts]
        for h in range(1, n_heads):
            out = out + r[h * Ts:(h + 1) * Ts]
        return out

    off = pl.multiple_of(p * page, page)
    keys_sc[:, pl.ds(off, page)] = _sort_key(scores(ikp_ref[0].astype(_MXU)))

    @pl.when(p == n_pages - 1)
    def _():
        ikn = jnp.concatenate([ikn_ref[...], jnp.zeros((page - Ts, hd), _F32)], axis=0).astype(_MXU)
        t_i = lax.broadcasted_iota(jnp.int32, (Ts, page), 0)
        j_i = lax.broadcasted_iota(jnp.int32, (Ts, page), 1)
        keys_sc[:, n_pages * page:(n_pages + 1) * page] = jnp.where(j_i <= t_i, _sort_key(scores(ikn)), _INT_MIN)
        keys = keys_sc[...]

        def count_ge(cand):
            return jnp.sum(jnp.where(keys >= cand, 1.0, 0.0), axis=1, keepdims=True)

        thr = jnp.maximum(_kth_largest_key(count_ge, (Ts, 1), topk), _INT_MIN + 1)
        mask_ref[0] = jnp.where(keys >= thr, 1.0, 0.0)


def _smp_index(z, cache_idx, page_table, *, row0, NB, Ts, topk, lay):
    n_heads, hd = lay["idx_heads"], lay["idx_hd"]
    n_pages = page_table.shape[1]
    page = cache_idx.shape[1]
    width = n_heads * hd
    rb0 = row0 // Ts
    Lp = (n_pages + 1) * page
    grid_spec = pltpu.PrefetchScalarGridSpec(
        num_scalar_prefetch=1,
        grid=(NB, n_pages),
        in_specs=[pl.BlockSpec((Ts, width), lambda n, p, pt: (rb0 + n, lay["iq"] // width)),
                  pl.BlockSpec((Ts, _LANES), lambda n, p, pt: (rb0 + n, lay["small"] // _LANES)),
                  pl.BlockSpec((Ts, hd), lambda n, p, pt: (rb0 + n, lay["ik"] // hd)),
                  pl.BlockSpec((1, page, hd), lambda n, p, pt: (pt[n * n_pages + p], 0, 0))],
        out_specs=pl.BlockSpec((1, Ts, Lp), lambda n, p, pt: (n, 0, 0)),
        scratch_shapes=[pltpu.VMEM((n_heads * Ts, hd), _F32), pltpu.VMEM((n_heads * Ts, page), _F32),
                        pltpu.VMEM((Ts, Lp), jnp.int32)],
    )
    return pl.pallas_call(
        functools.partial(_smp_index_kernel, n_pages=n_pages, page=page, Ts=Ts, topk=topk, n_heads=n_heads,
                          hd=hd, iw_lane0=lay["iw_lane0"]),
        grid_spec=grid_spec,
        out_shape=jax.ShapeDtypeStruct((NB, Ts, Lp), _F32),
        compiler_params=_cparams(("parallel", "arbitrary")),
        name="smp_index_topk",
    )(page_table.reshape(-1), z, z, z, cache_idx)


def _smp_attn_kernel(pt_ref, q_ref, kn_ref, vn_ref, mask_ref, kp_ref, vp_ref, o_ref, qbd_sc, m_sc, l_sc, acc_sc, *,
                     n_pages, page, Ts, n_heads, hd):
    p = pl.program_id(1)

    @pl.when(p == 0)
    def _():
        qbd_sc[...] = jnp.zeros_like(qbd_sc)
        for h in range(n_heads):
            qbd_sc[h * Ts:(h + 1) * Ts, h * hd:(h + 1) * hd] = q_ref[:, h * hd:(h + 1) * hd]
        m_sc[...] = jnp.full_like(m_sc, _NEG)
        l_sc[...] = jnp.zeros_like(l_sc)
        acc_sc[...] = jnp.zeros_like(acc_sc)

    qbd = qbd_sc[...].astype(_MXU)

    def step(kblk, vblk, mk_q):
        s = _dot_nt(qbd, kblk) * (hd ** -0.5)
        mk = jnp.concatenate([mk_q] * n_heads, axis=0) > 0.5
        s = jnp.where(mk, s, _NEG)
        m_prev = m_sc[...]
        m_new = jnp.maximum(m_prev, jnp.max(s, axis=1, keepdims=True))
        alpha = jnp.exp(m_prev - m_new)
        pr = jnp.exp(s - m_new)
        l_sc[...] = alpha * l_sc[...] + jnp.sum(pr, axis=1, keepdims=True)
        acc_sc[...] = alpha * acc_sc[...] + _dot(pr.astype(_MXU), vblk)
        m_sc[...] = m_new

    off = pl.multiple_of(p * page, page)
    step(kp_ref[0].astype(_MXU), vp_ref[0].astype(_MXU), mask_ref[0, :, pl.ds(off, page)])

    @pl.when(p == n_pages - 1)
    def _():
        pad = jnp.zeros((page - Ts, n_heads * hd), _F32)
        kn = jnp.concatenate([kn_ref[...], pad], axis=0).astype(_MXU)
        vn = jnp.concatenate([vn_ref[...], pad], axis=0).astype(_MXU)
        step(kn, vn, mask_ref[0, :, n_pages * page:(n_pages + 1) * page])
        for h in range(n_heads):
            rs = slice(h * Ts, (h + 1) * Ts)
            cs = slice(h * hd, (h + 1) * hd)
            o_ref[:, cs] = acc_sc[rs, cs] / l_sc[rs, :]


def _smp_attn(z, mask, cache_k, cache_v, page_table, *, row0, NB, Ts, lay):
    n_heads, hd = lay["dsa_heads"], lay["dsa_hd"]
    n_pages = page_table.shape[1]
    page = cache_k.shape[1]
    width = n_heads * hd
    rb0 = row0 // Ts
    Lp = (n_pages + 1) * page
    grid_spec = pltpu.PrefetchScalarGridSpec(
        num_scalar_prefetch=1,
        grid=(NB, n_pages),
        in_specs=[pl.BlockSpec((Ts, width), lambda n, p, pt: (rb0 + n, lay["qd"] // width)),
                  pl.BlockSpec((Ts, width), lambda n, p, pt: (rb0 + n, lay["kd"] // width)),
                  pl.BlockSpec((Ts, width), lambda n, p, pt: (rb0 + n, lay["vd"] // width)),
                  pl.BlockSpec((1, Ts, Lp), lambda n, p, pt: (n, 0, 0)),
                  pl.BlockSpec((1, page, width), lambda n, p, pt: (pt[n * n_pages + p], 0, 0)),
                  pl.BlockSpec((1, page, width), lambda n, p, pt: (pt[n * n_pages + p], 0, 0))],
        out_specs=pl.BlockSpec((Ts, width), lambda n, p, pt: (n, 0)),
        scratch_shapes=[pltpu.VMEM((n_heads * Ts, width), _F32), pltpu.VMEM((n_heads * Ts, 1), _F32),
                        pltpu.VMEM((n_heads * Ts, 1), _F32), pltpu.VMEM((n_heads * Ts, width), _F32)],
    )
    return pl.pallas_call(
        functools.partial(_smp_attn_kernel, n_pages=n_pages, page=page, Ts=Ts, n_heads=n_heads, hd=hd),
        grid_spec=grid_spec,
        out_shape=jax.ShapeDtypeStruct((NB * Ts, width), _F32),
        compiler_params=_cparams(("parallel", "arbitrary")),
        name="smp_attn",
    )(page_table.reshape(-1), z, z, z, mask, cache_k, cache_v)


def _merge_kernel(a_ref, b_ref, wg_ref, wd_ref, ga_ref, gb_ref, o_ref, a_sc, b_sc):
    @pl.when(pl.program_id(1) == 0)
    def _():
        a_sc[...] = a_ref[...].astype(a_sc.dtype)
        b_sc[...] = b_ref[...].astype(b_sc.dtype)

    ya = _dot(a_sc[...], wg_ref[...])
    yb = _dot(b_sc[...], wd_ref[...])
    o_ref[...] = (jax.nn.sigmoid(ga_ref[...]) * ya + jax.nn.sigmoid(gb_ref[...]) * yb).astype(o_ref.dtype)


def _merge(a, b, wg, wd, z, *, lay):
    M, K = a.shape
    N = wg.shape[1]
    tm = _pick(M, (512, 256, 128))
    tn = _pick(N, (512, 256, 128))
    return pl.pallas_call(
        _merge_kernel,
        grid=(M // tm, N // tn),
        in_specs=[pl.BlockSpec((tm, K), lambda i, j: (i, 0)),
                  pl.BlockSpec((tm, K), lambda i, j: (i, 0)),
                  pl.BlockSpec((K, tn), lambda i, j: (0, j)),
                  pl.BlockSpec((K, tn), lambda i, j: (0, j)),
                  pl.BlockSpec((tm, tn), lambda i, j: (i, lay["ga"] // tn + j)),
                  pl.BlockSpec((tm, tn), lambda i, j: (i, lay["gb"] // tn + j))],
        out_specs=pl.BlockSpec((tm, tn), lambda i, j: (i, j)),
        out_shape=jax.ShapeDtypeStruct((M, N), _MXU),
        scratch_shapes=[pltpu.VMEM((tm, K), _MXU), pltpu.VMEM((tm, K), _MXU)],
        compiler_params=_cparams(("parallel", "arbitrary")),
        name="branch_merge",
    )(a, b, wg, wd, z, z)


def _out_proj_kernel(m_ref, w_ref, x_ref, g_ref, h_ref, hn_ref, *, tn, n_j):
    j = pl.program_id(1)
    off = pl.multiple_of(j * tn, tn)
    h_ref[:, pl.ds(off, tn)] = x_ref[...] + _dot(m_ref[...], w_ref[...])

    @pl.when(j == n_j - 1)
    def _():
        hn_ref[...] = _rms(h_ref[...], g_ref[...]).astype(hn_ref.dtype)


def _out_proj(m, w, x, gain):
    M, K = m.shape
    N = w.shape[1]
    tm = _pick(M, (512, 256, 128))
    tn = _pick(N, (512, 256, 128))
    return pl.pallas_call(
        functools.partial(_out_proj_kernel, tn=tn, n_j=N // tn),
        grid=(M // tm, N // tn),
        in_specs=[pl.BlockSpec((tm, K), lambda i, j: (i, 0)),
                  pl.BlockSpec((K, tn), lambda i, j: (0, j)),
                  pl.BlockSpec((tm, tn), lambda i, j: (i, j)),
                  pl.BlockSpec((1, N), lambda i, j: (0, 0))],
        out_specs=[pl.BlockSpec((tm, N), lambda i, j: (i, 0)),
                   pl.BlockSpec((tm, N), lambda i, j: (i, 0))],
        out_shape=[jax.ShapeDtypeStruct((M, N), _F32), jax.ShapeDtypeStruct((M, N), _MXU)],
        compiler_params=_cparams(("parallel", "arbitrary")),
        name="out_proj_norm",
    )(m, w, x, gain)


def _conv_gate(ua, ub, sa, sb, wca_ref, wcb_ref, bca_ref, bcb_ref):
    ca = sa[0] * wca_ref[0:1, :] + sa[1] * wca_ref[1:2, :] + ua * wca_ref[2:3, :] + bca_ref[...]
    cb = sb[0] * wcb_ref[0:1, :] + sb[1] * wcb_ref[1:2, :] + ub * wcb_ref[2:3, :] + bcb_ref[...]
    return jax.nn.gelu(ca, approximate=True) * cb


def _ffn_up_seq_kernel(hn_ref, wa_ref, wb_ref, wca_ref, wcb_ref, bca_ref, bcb_ref, g_ref, ta_ref, tb_ref,
                       ca_sc, cb_sc, *, tm):
    @pl.when(pl.program_id(1) == 0)
    def _():
        ca_sc[...] = jnp.zeros_like(ca_sc)
        cb_sc[...] = jnp.zeros_like(cb_sc)

    x = hn_ref[...]
    ua = _dot(x, wa_ref[...])
    ub = _dot(x, wb_ref[...])

    def shifted(u, carry_sc):
        ext = jnp.concatenate([carry_sc[...], u], axis=0)
        s1 = pltpu.roll(ext, 1, axis=0)[8:]
        s2 = pltpu.roll(ext, 2, axis=0)[8:]
        carry_sc[...] = u[tm - 8:]
        return s2, s1

    g_ref[...] = _conv_gate(ua, ub, shifted(ua, ca_sc), shifted(ub, cb_sc),
                            wca_ref, wcb_ref, bca_ref, bcb_ref).astype(g_ref.dtype)
    ta_ref[...] = ua[tm - 8:]
    tb_ref[...] = ub[tm - 8:]


def _ffn_up_seq(hn, w_up, w_conv, b_conv, *, T, ffp):
    K = hn.shape[1]
    tm = _pick(T, (512, 256, 128))
    tn = _pick(ffp, (512, 256, 128))
    nj = ffp // tn
    return pl.pallas_call(
        functools.partial(_ffn_up_seq_kernel, tm=tm),
        grid=(nj, T // tm),
        in_specs=[pl.BlockSpec((tm, K), lambda j, i: (i, 0)),
                  pl.BlockSpec((K, tn), lambda j, i: (0, j)),
                  pl.BlockSpec((K, tn), lambda j, i: (0, nj + j)),
                  pl.BlockSpec((3, tn), lambda j, i: (0, j)),
                  pl.BlockSpec((3, tn), lambda j, i: (0, nj + j)),
                  pl.BlockSpec((1, tn), lambda j, i: (0, j)),
                  pl.BlockSpec((1, tn), lambda j, i: (0, nj + j))],
        out_specs=[pl.BlockSpec((tm, tn), lambda j, i: (i, j)),
                   pl.BlockSpec((8, tn), lambda j, i: (0, j)),
                   pl.BlockSpec((8, tn), lambda j, i: (0, j))],
        out_shape=[jax.ShapeDtypeStruct((T, ffp), _MXU),
                   jax.ShapeDtypeStruct((8, ffp), _F32), jax.ShapeDtypeStruct((8, ffp), _F32)],
        scratch_shapes=[pltpu.VMEM((8, tn), _F32), pltpu.VMEM((8, tn), _F32)],
        compiler_params=_cparams(("parallel", "arbitrary")),
        name="ffn_up_conv_prompt",
    )(hn, w_up, w_up, w_conv, w_conv, b_conv, b_conv)


def _ffn_up_dec_kernel(hn_ref, wa_ref, wb_ref, wca_ref, wcb_ref, bca_ref, bcb_ref, e1a_ref, e1b_ref, e2a_ref,
                       e2b_ref, g_ref, ua_ref, ub_ref, *, Ts):
    x = hn_ref[...]
    ua = _dot(x, wa_ref[...])
    ub = _dot(x, wb_ref[...])
    r = lax.broadcasted_iota(jnp.int32, ua.shape, 0) % Ts

    def shifted(u, e1_ref, e2_ref):
        s1 = jnp.where(r == 0, e1_ref[...], pltpu.roll(u, 1, axis=0))
        s2 = jnp.where(r < 2, e2_ref[...], pltpu.roll(u, 2, axis=0))
        return s2, s1

    g_ref[...] = _conv_gate(ua, ub, shifted(ua, e1a_ref, e2a_ref), shifted(ub, e1b_ref, e2b_ref),
                            wca_ref, wcb_ref, bca_ref, bcb_ref).astype(g_ref.dtype)
    ua_ref[...] = ua
    ub_ref[...] = ub


def _ffn_up_dec(hn, w_up, w_conv, b_conv, e1, e2, *, row0, R, Ts, ffp):
    K = hn.shape[1]
    tn = _pick(ffp, (512, 256, 128))
    nj = ffp // tn
    rb0 = row0 // R
    return pl.pallas_call(
        functools.partial(_ffn_up_dec_kernel, Ts=Ts),
        grid=(nj,),
        in_specs=[pl.BlockSpec((R, K), lambda j: (rb0, 0)),
                  pl.BlockSpec((K, tn), lambda j: (0, j)),
                  pl.BlockSpec((K, tn), lambda j: (0, nj + j)),
                  pl.BlockSpec((3, tn), lambda j: (0, j)),
                  pl.BlockSpec((3, tn), lambda j: (0, nj + j)),
                  pl.BlockSpec((1, tn), lambda j: (0, j)),
                  pl.BlockSpec((1, tn), lambda j: (0, nj + j)),
                  pl.BlockSpec((R, tn), lambda j: (0, j)),
                  pl.BlockSpec((R, tn), lambda j: (0, nj + j)),
                  pl.BlockSpec((R, tn), lambda j: (0, j)),
                  pl.BlockSpec((R, tn), lambda j: (0, nj + j))],
        out_specs=[pl.BlockSpec((R, tn), lambda j: (0, j)),
                   pl.BlockSpec((R, tn), lambda j: (0, j)),
                   pl.BlockSpec((R, tn), lambda j: (0, j))],
        out_shape=[jax.ShapeDtypeStruct((R, ffp), _MXU),
                   jax.ShapeDtypeStruct((R, ffp), _F32), jax.ShapeDtypeStruct((R, ffp), _F32)],
        compiler_params=_cparams(("parallel",)),
        name="ffn_up_conv_sample",
    )(hn, w_up, w_up, w_conv, w_conv, b_conv, b_conv, e1, e1, e2, e2)


def _ffn_down_kernel(g_ref, w_ref, h_ref, gn_ref, y_ref, acc_sc, *, n_k):
    k = pl.program_id(1)

    @pl.when(k == 0)
    def _():
        acc_sc[...] = jnp.zeros_like(acc_sc)

    acc_sc[...] += _dot(g_ref[...], w_ref[...])

    @pl.when(k == n_k - 1)
    def _():
        y_ref[...] = _rms(h_ref[...] + acc_sc[...], gn_ref[...])


def _ffn_down(g, w, h, gain):
    M, K = g.shape
    N = w.shape[1]
    tm = _pick(M, (512, 256, 128))
    tk = _pick(K, (512, 256, 128))
    return pl.pallas_call(
        functools.partial(_ffn_down_kernel, n_k=K // tk),
        grid=(M // tm, K // tk),
        in_specs=[pl.BlockSpec((tm, tk), lambda i, k: (i, k)),
                  pl.BlockSpec((tk, N), lambda i, k: (k, 0)),
                  pl.BlockSpec((tm, N), lambda i, k: (i, 0)),
                  pl.BlockSpec((1, N), lambda i, k: (0, 0))],
        out_specs=pl.BlockSpec((tm, N), lambda i, k: (i, 0)),
        out_shape=jax.ShapeDtypeStruct((M, N), _F32),
        scratch_shapes=[pltpu.VMEM((tm, N), _F32)],
        compiler_params=_cparams(("parallel", "arbitrary")),
        name="ffn_down_norm",
    )(g, w, h, gain)


def _pad_cols(a, n):
    return a if a.shape[-1] == n else jnp.pad(a, [(0, 0)] * (a.ndim - 1) + [(0, n - a.shape[-1])])


def kernel(x_prompt, x_sample, cache_k, cache_v, cache_idx_k, state_gla, state_ffn_conv, page_table, norm_mix, w_in, w_a2, b_a, gla_norm, w_gla_proj, w_dsa_proj, w_out, norm_ffn, w_up, w_conv, b_conv, w_down, norm_final):
    depth = w_in.shape[0]
    assert depth == 1, "single-layer trunk"
    B, T, D = x_prompt.shape
    NB, Ts, _ = x_sample.shape
    assert B == 1 and Ts == 8
    _, _, GH, dk, dv = state_gla.shape
    rank = w_a2.shape[1]
    _, n_pool, page, DH, dhd = cache_k.shape
    ihd = cache_idx_k.shape[-1]
    n_mix = w_in.shape[-1]
    IH = (n_mix - (2 * GH * dk + 2 * GH * dv + rank + 3 * DH * dhd + ihd + 2 * D)) // (ihd + 1)
    ff = w_down.shape[1]
    n_pages = page_table.shape[1]
    past = n_pages * page
    assert page == _LANES and ihd == _LANES and dhd == _LANES and 2 * rank <= _LANES and IH + rank <= _LANES

    sizes = (GH * dk, GH * dk, GH * dv, GH * dv, rank, DH * dhd, DH * dhd, DH * dhd, IH * ihd, IH, ihd, 2 * D)
    o = np.concatenate([[0], np.cumsum(sizes)])
    assert o[-1] == n_mix
    lay = dict(gla_heads=GH, gla_dk=dk, gla_dv=dv, dsa_heads=DH, dsa_hd=dhd, idx_heads=IH, idx_hd=ihd,
               iw_lane0=rank)
    segs, pos = [], 0
    for name, lo, hi in (("qg", o[0], o[1]), ("kg", o[1], o[2]), ("vg", o[2], o[3]), ("rg", o[3], o[4]),
                         ("qd", o[5], o[6]), ("kd", o[6], o[7]), ("vd", o[7], o[8]), ("iq", o[8], o[9]),
                         ("ga", o[11], o[11] + D), ("gb", o[11] + D, o[12]), ("ik", o[10], o[11])):
        lay[name] = pos
        segs.append((int(lo), int(hi)))
        pos += int(hi - lo)
    lay["small"] = pos
    nz = -(-(pos + _LANES) // 512) * 512
    w0 = w_in[0].astype(_MXU)
    w_in_p = jnp.concatenate([w0[:, lo:hi] for lo, hi in segs]
                             + [w0[:, o[4]:o[5]], w0[:, o[9]:o[10]],
                                jnp.zeros((D, nz - pos - rank - IH), _MXU)], axis=1)
    wa_p = jnp.zeros((_LANES, GH * dk), _MXU).at[:rank].set(w_a2[0].astype(_MXU))

    x_all = jnp.concatenate([x_prompt[0], x_sample.reshape(NB * Ts, D)], axis=0)
    z, zb = _norm_matmul(x_all, norm_mix, w_in_p)

    Cp = _pick(T, (128, 64, 32, 16, 8))
    og_p, S_p = _gla(z, wa_p, b_a, gla_norm, None, row0=0, n_seq=1, n_chunks=T // Cp, C=Cp, lay=lay)
    og_s, S_s = _gla(z, wa_p, b_a, gla_norm, state_gla[0], row0=T, n_seq=NB, n_chunks=1, C=Ts, lay=lay)

    mask_p = _dsa_index(z, zb, T=T, topk=min(_TOPK_MAX, T // 4), lay=lay)
    od_p = _dsa_attn(zb, mask_p, T=T, lay=lay)
    mask_s = _smp_index(z, cache_idx_k[0], page_table, row0=T, NB=NB, Ts=Ts,
                        topk=min(_TOPK_MAX, (past + Ts) // 4), lay=lay)
    od_s = _smp_attn(z, mask_s, cache_k[0].reshape(n_pool, page, DH * dhd), cache_v[0].reshape(n_pool, page, DH * dhd),
                     page_table, row0=T, NB=NB, Ts=Ts, lay=lay)

    mixed = _merge(jnp.concatenate([og_p, og_s], axis=0), jnp.concatenate([od_p, od_s], axis=0),
                   w_gla_proj[0].astype(_MXU), w_dsa_proj[0].astype(_MXU), z, lay=lay)
    h, hn = _out_proj(mixed, w_out[0].astype(_MXU), x_all, norm_ffn)

    ffp = -(-ff // 512) * 512
    w_up_p = jnp.concatenate([_pad_cols(w_up[0, :, :ff].astype(_MXU), ffp),
                              _pad_cols(w_up[0, :, ff:].astype(_MXU), ffp)], axis=1)
    w_conv_p = jnp.concatenate([_pad_cols(w_conv[0, :, :ff], ffp), _pad_cols(w_conv[0, :, ff:], ffp)], axis=1)
    b_conv_p = jnp.concatenate([_pad_cols(b_conv[:, :ff], ffp), _pad_cols(b_conv[:, ff:], ffp)], axis=1)
    w_down_p = jnp.pad(w_down[0].astype(_MXU), ((0, ffp - ff), (0, 0)))
    g_p, ta, tb = _ffn_up_seq(hn, w_up_p, w_conv_p, b_conv_p, T=T, ffp=ffp)
    prev = state_ffn_conv[0]
    prev = jnp.concatenate([_pad_cols(prev[..., :ff], ffp), _pad_cols(prev[..., ff:], ffp)], axis=-1)
    R = NB * Ts
    e1 = jnp.zeros((NB, Ts, 2 * ffp), _F32).at[:, 0].set(prev[:, 1]).reshape(R, 2 * ffp)
    e2 = jnp.zeros((NB, Ts, 2 * ffp), _F32).at[:, 0].set(prev[:, 0]).at[:, 1].set(prev[:, 1]).reshape(R, 2 * ffp)
    g_s, ua_s, ub_s = _ffn_up_dec(hn, w_up_p, w_conv_p, b_conv_p, e1, e2, row0=T, R=R, Ts=Ts, ffp=ffp)
    y = _ffn_down(jnp.concatenate([g_p, g_s], axis=0), w_down_p, h, norm_final.reshape(1, D))

    def seg(name, width, rows):
        return z[rows, lay[name]:lay[name] + width]

    rp, rs = slice(0, T), slice(T, T + R)
    conv_p = jnp.concatenate([ta[6:8, :ff], tb[6:8, :ff]], axis=-1)
    conv_s = jnp.concatenate([ua_s.reshape(NB, Ts, ffp)[:, Ts - 2:, :ff], ub_s.reshape(NB, Ts, ffp)[:, Ts - 2:, :ff]],
                             axis=-1)
    return (y[rp].reshape(1, T, D), y[rs].reshape(NB, Ts, D),
            seg("kd", DH * dhd, rp).reshape(1, 1, T, DH, dhd), seg("vd", DH * dhd, rp).reshape(1, 1, T, DH, dhd),
            seg("ik", ihd, rp).reshape(1, 1, T, ihd),
            S_p.reshape(1, 1, GH, dk, dv), conv_p.reshape(1, 1, 2, 2 * ff),
            seg("kd", DH * dhd, rs).reshape(1, NB, Ts, DH, dhd), seg("vd", DH * dhd, rs).reshape(1, NB, Ts, DH, dhd),
            seg("ik", ihd, rs).reshape(1, NB, Ts, ihd),
            S_s.reshape(1, NB, GH, dk, dv), conv_s.reshape(1, NB, 2, 2 * ff))
```

```python
import functools

import jax
import jax.numpy as jnp
import numpy as np
from jax import lax
from jax.experimental import pallas as pl
from jax.experimental.pallas import tpu as pltpu

_F32 = jnp.float32
_MXU = jnp.bfloat16
_EPS = 1e-6
_GLA_TAU = 16.0
_GLA_SUB = 16
_GLA_PAD = 128
_TOPK_MAX = 256
_LANES = 128
_NEG = -1e30
_LOG2E = 1.4426950408889634
_INT_MIN = -(2 ** 31)
_VMEM_LIMIT = 56 * 1024 * 1024


def _cparams(sem, flags=None):
    return pltpu.CompilerParams(dimension_semantics=sem, vmem_limit_bytes=_VMEM_LIMIT, flags=flags)


def _pick(n, cands):
    for c in cands:
        if n % c == 0:
            return c
    raise ValueError(f"no tile in {cands} divides {n}")


def _dot(a, b):
    return jnp.dot(a, b, preferred_element_type=_F32)


def _dot_nt(a, b):
    return lax.dot_general(a, b, (((1,), (1,)), ((), ())), preferred_element_type=_F32)


def _rms(x, g):
    return x * lax.rsqrt(jnp.mean(x * x, axis=-1, keepdims=True) + _EPS) * g


def _sort_key(x):
    bits = lax.bitcast_convert_type(x, jnp.int32)
    return bits ^ ((bits >> 31) & jnp.int32(0x7FFFFFFF))


def _kth_largest_key(count_ge, shape, k):
    kf = float(k)
    lo = jnp.where(count_ge(jnp.zeros(shape, jnp.int32)) >= kf, 0, _INT_MIN).astype(jnp.int32)

    def step(it, lo):
        cand = lo + (jnp.int32(1) << (30 - it))
        return jnp.where(count_ge(cand) >= kf, cand, lo)

    return lax.fori_loop(0, 31, step, lo)


def _norm_matmul_kernel(x_ref, g_ref, w_ref, o_ref, ob_ref, xn_sc):
    @pl.when(pl.program_id(1) == 0)
    def _():
        xn_sc[...] = _rms(x_ref[...], g_ref[...]).astype(xn_sc.dtype)

    acc = _dot(xn_sc[...], w_ref[...])
    o_ref[...] = acc
    ob_ref[...] = acc.astype(ob_ref.dtype)


def _norm_matmul(x, gain, w):
    M, K = x.shape
    N = w.shape[1]
    tm = _pick(M, (1024, 512, 256, 128))
    tn = _pick(N, (512, 256, 128))
    return pl.pallas_call(
        _norm_matmul_kernel,
        grid=(M // tm, N // tn),
        in_specs=[pl.BlockSpec((tm, K), lambda i, j: (i, 0)),
                  pl.BlockSpec((1, K), lambda i, j: (0, 0)),
                  pl.BlockSpec((K, tn), lambda i, j: (0, j))],
        out_specs=[pl.BlockSpec((tm, tn), lambda i, j: (i, j)),
                   pl.BlockSpec((tm, tn), lambda i, j: (i, j))],
        out_shape=[jax.ShapeDtypeStruct((M, N), _F32), jax.ShapeDtypeStruct((M, N), _MXU)],
        scratch_shapes=[pltpu.VMEM((tm, K), _MXU)],
        compiler_params=_cparams(("parallel", "arbitrary")),
        name="norm_in_proj",
    )(x, gain, w)


def _gla_kernel(*refs, C, has_init, dk):
    if has_init:
        q_ref, k_ref, v_ref, r_ref, a_ref, wa_ref, ba_ref, gn_ref, s0_ref, o_ref, so_ref, S_sc, b_sc = refs
    else:
        q_ref, k_ref, v_ref, r_ref, a_ref, wa_ref, ba_ref, gn_ref, o_ref, so_ref, S_sc, b_sc = refs
    P = _GLA_PAD
    c = pl.program_id(2)

    @pl.when(c == 0)
    def _():
        if has_init:
            S_sc[...] = s0_ref[0, 0]
        else:
            S_sc[...] = jnp.zeros_like(S_sc)

    def padded(x):
        if C == P:
            return x
        return jnp.concatenate([x, jnp.zeros((P - C, x.shape[1]), x.dtype)], axis=0)

    q = padded(q_ref[...]) * (dk ** -0.5)
    k = padded(k_ref[...])
    v = padded(v_ref[...]).astype(_MXU)
    zz = _dot(padded(a_ref[...]).astype(_MXU), wa_ref[...]) + ba_ref[...]
    la = (jnp.minimum(zz, 0.0) - jnp.log(1.0 + jnp.exp(-jnp.abs(zz)))) / _GLA_TAU
    row = lax.broadcasted_iota(jnp.int32, la.shape, 0)
    if C < P:
        la = jnp.where(row < C, la, 0.0)
    b = la
    sh = 1
    while sh < P:
        b = b + jnp.where(row >= sh, pltpu.roll(b, sh, axis=0), 0.0)
        sh *= 2
    b_sc[...] = b

    S = S_sc[...]
    o = _dot((q * jnp.exp(b)).astype(_MXU), S.astype(_MXU))
    n_sub = -(-C // _GLA_SUB)
    krow = lax.broadcasted_iota(jnp.int32, (P, dk), 0)
    a_rows = []
    for i in range(n_sub):
        lo, hi = i * _GLA_SUB, (i + 1) * _GLA_SUB
        anchor = b_sc[pl.ds(lo - 1, 1), :] if i > 0 else jnp.zeros((1, dk), _F32)
        qt = (q[lo:hi] * jnp.exp(b[lo:hi] - anchor)).astype(_MXU)
        kt = (k * jnp.exp(jnp.where(krow < hi, anchor - b, 0.0))).astype(_MXU)
        a_i = _dot_nt(qt, kt)
        r_i = lax.broadcasted_iota(jnp.int32, a_i.shape, 0) + lo
        c_i = lax.broadcasted_iota(jnp.int32, a_i.shape, 1)
        a_rows.append(jnp.where(c_i <= r_i, a_i, 0.0))
    if n_sub * _GLA_SUB < P:
        a_rows.append(jnp.zeros((P - n_sub * _GLA_SUB, P), _F32))
    A = jnp.concatenate(a_rows, axis=0)
    o = o + _dot(A.astype(_MXU), v)
    bT = b.T
    b_last = bT[:, P - 1:P]
    kdT = (k.T * jnp.exp(b_last - bT)).astype(_MXU)
    S_new = jnp.exp(b_last) * S + _dot(kdT, v)
    S_sc[...] = S_new
    so_ref[0, 0] = S_new

    o = o[:C]
    o = _rms(o, gn_ref[...])
    r = r_ref[...]
    o_ref[...] = o * (r * jax.nn.sigmoid(r))


def _gla(z, wa, ba, gn, s0, *, row0, n_seq, n_chunks, C, lay):
    H, dk, dv = lay["gla_heads"], lay["gla_dk"], lay["gla_dv"]
    rb0 = row0 // C

    def rows(s, c):
        return rb0 + s * n_chunks + c

    in_specs = [
        pl.BlockSpec((C, dk), lambda s, h, c: (rows(s, c), lay["qg"] // dk + h)),
        pl.BlockSpec((C, dk), lambda s, h, c: (rows(s, c), lay["kg"] // dk + h)),
        pl.BlockSpec((C, dv), lambda s, h, c: (rows(s, c), lay["vg"] // dv + h)),
        pl.BlockSpec((C, dv), lambda s, h, c: (rows(s, c), lay["rg"] // dv + h)),
        pl.BlockSpec((C, _LANES), lambda s, h, c: (rows(s, c), lay["small"] // _LANES)),
        pl.BlockSpec((_LANES, dk), lambda s, h, c: (0, h)),
        pl.BlockSpec((1, dk), lambda s, h, c: (0, h)),
        pl.BlockSpec((1, dv), lambda s, h, c: (0, h)),
    ]
    args = [z, z, z, z, z, wa, ba, gn]
    if s0 is not None:
        in_specs.append(pl.BlockSpec((1, 1, dk, dv), lambda s, h, c: (s, h, 0, 0)))
        args.append(s0)
    return pl.pallas_call(
        functools.partial(_gla_kernel, C=C, has_init=s0 is not None, dk=dk),
        grid=(n_seq, H, n_chunks),
        in_specs=in_specs,
        out_specs=[pl.BlockSpec((C, dv), lambda s, h, c: (s * n_chunks + c, h)),
                   pl.BlockSpec((1, 1, dk, dv), lambda s, h, c: (s, h, 0, 0))],
        out_shape=[jax.ShapeDtypeStruct((n_seq * n_chunks * C, H * dv), _F32),
                   jax.ShapeDtypeStruct((n_seq, H, dk, dv), _F32)],
        scratch_shapes=[pltpu.VMEM((dk, dv), _F32), pltpu.VMEM((_GLA_PAD, dk), _F32)],
        compiler_params=_cparams(("parallel", "parallel", "arbitrary")),
        name="gla",
    )(*args)


def _dsa_index_kernel(iq_ref, iwT_ref, ik_ref, maskT_ref, keys_sc, *, tq, tk, topk, n_heads, hd, n_ktiles):
    qi = pl.program_id(0)
    per_q = tq // tk
    nk = (qi + 1) * per_q
    wT = iwT_ref[...] * ((n_heads ** -0.5) * (hd ** -0.5))
    qpos = qi * tq + lax.broadcasted_iota(jnp.int32, (tk, tq), 1)
    krow = lax.broadcasted_iota(jnp.int32, (tk, tq), 0)

    def score_tile(kj, carry):
        off = pl.multiple_of(kj * tk, tk)
        ik = ik_ref[pl.ds(off, tk), :]
        acc = jnp.zeros((tk, tq), _F32)
        for h in range(n_heads):
            s = _dot_nt(ik, iq_ref[:, h * hd:(h + 1) * hd])
            acc = acc + jnp.maximum(s, 0.0) * wT[h:h + 1, :]
        keys_sc[pl.ds(off, tk), :] = jnp.where(krow + off <= qpos, _sort_key(acc), _INT_MIN)
        return carry

    lax.fori_loop(0, nk, score_tile, 0)

    def count_ge(cand):
        def body(kq, acc):
            off = pl.multiple_of(kq * tq, tq)
            hit = jnp.where(keys_sc[pl.ds(off, tq), :] >= cand, 1.0, 0.0)
            for r in range(tq // 8):
                acc = acc + hit[r * 8:(r + 1) * 8]
            return acc

        acc = lax.fori_loop(0, qi + 1, body, jnp.zeros((8, tq), _F32))
        return jnp.sum(acc, axis=0, keepdims=True)

    thr = jnp.maximum(_kth_largest_key(count_ge, (1, tq), topk), _INT_MIN + 1)

    def write_mask(kj, carry):
        off = pl.multiple_of(kj * tk, tk)
        maskT_ref[pl.ds(off, tk), :] = jnp.where(keys_sc[pl.ds(off, tk), :] >= thr, 1.0, 0.0).astype(maskT_ref.dtype)
        return carry

    lax.fori_loop(0, nk, write_mask, 0)

    def write_zero(kj, carry):
        off = pl.multiple_of(kj * tk, tk)
        maskT_ref[pl.ds(off, tk), :] = jnp.zeros((tk, tq), maskT_ref.dtype)
        return carry

    lax.fori_loop(nk, n_ktiles, write_zero, 0)


def _dsa_index(zb, iwT, *, T, topk, lay):
    n_heads, hd = lay["idx_heads"], lay["idx_hd"]
    tq = _pick(T, (256, 128))
    tk = 128
    width = n_heads * hd
    return pl.pallas_call(
        functools.partial(_dsa_index_kernel, tq=tq, tk=tk, topk=topk, n_heads=n_heads, hd=hd, n_ktiles=T // tk),
        grid=(T // tq,),
        in_specs=[pl.BlockSpec((tq, width), lambda i: (i, lay["iq"] // width)),
                  pl.BlockSpec((n_heads, tq), lambda i: (0, i)),
                  pl.BlockSpec((T, hd), lambda i: (0, lay["ik"] // hd))],
        out_specs=pl.BlockSpec((T, tq), lambda i: (0, i)),
        out_shape=jax.ShapeDtypeStruct((T, T), jnp.bfloat16),
        scratch_shapes=[pltpu.VMEM((T, tq), jnp.int32)],
        compiler_params=_cparams(("parallel",)),
        name="dsa_index_topk",
    )(zb, iwT, zb)


def _dsa_attn_kernel(qi_tab, kj_tab, q_ref, k_ref, vT_ref, maskT_ref, o_ref, m_sc, l_sc, acc_sc, bias_sc, *,
                     n_heads, hd):
    p = pl.program_id(0)
    qi = qi_tab[p]
    kj = kj_tab[p]

    @pl.when(kj == 0)
    def _():
        m_sc[...] = jnp.full_like(m_sc, _NEG)
        l_sc[...] = jnp.zeros_like(l_sc)
        acc_sc[...] = jnp.zeros_like(acc_sc)

    bias_sc[...] = jnp.where(maskT_ref[...].astype(_F32) > 0.5, 0.0, _NEG)
    c = (hd ** -0.5) * _LOG2E

    def logits(h):
        sl = slice(h * hd, (h + 1) * hd)
        return _dot_nt(k_ref[:, sl], q_ref[:, sl])

    ahead = 5
    pending = [logits(h) for h in range(min(ahead, n_heads))]
    for h in range(n_heads):
        sl = slice(h * hd, (h + 1) * hd)
        if h + ahead < n_heads:
            pending.append(logits(h + ahead))
        u = pending[h] * c + bias_sc[...]
        m_prev = m_sc[h, 0:1, :]
        m_new = jnp.maximum(m_prev, jnp.max(u, axis=0, keepdims=True))
        alpha = jnp.exp2(m_prev - m_new)
        pr = jnp.exp2(u - m_new)
        l_new = alpha * l_sc[h, 0:1, :] + jnp.sum(pr, axis=0, keepdims=True)
        acc_sc[sl, :] = alpha * acc_sc[sl, :] + _dot(vT_ref[sl, :], pr.astype(_MXU))
        m_sc[h] = jnp.broadcast_to(m_new, m_sc.shape[1:])
        l_sc[h] = jnp.broadcast_to(l_new, l_sc.shape[1:])

    @pl.when(kj == qi)
    def _():
        for h in range(n_heads):
            sl = slice(h * hd, (h + 1) * hd)
            o_ref[:, sl] = (acc_sc[sl, :] / l_sc[h, 0:1, :]).T


def _dsa_attn(zb, vT, maskT, *, T, lay):
    n_heads, hd = lay["dsa_heads"], lay["dsa_hd"]
    width = n_heads * hd
    t = _pick(T, (256, 128))
    nq = T // t
    qi_tab = np.concatenate([np.full(i + 1, i, np.int32) for i in range(nq)])
    kj_tab = np.concatenate([np.arange(i + 1, dtype=np.int32) for i in range(nq)])
    grid_spec = pltpu.PrefetchScalarGridSpec(
        num_scalar_prefetch=2,
        grid=(len(qi_tab),),
        in_specs=[pl.BlockSpec((t, width), lambda p, qt, kt: (qt[p], lay["qd"] // width)),
                  pl.BlockSpec((t, width), lambda p, qt, kt: (kt[p], lay["kd"] // width)),
                  pl.BlockSpec((width, t), lambda p, qt, kt: (0, kt[p])),
                  pl.BlockSpec((t, t), lambda p, qt, kt: (kt[p], qt[p]))],
        out_specs=pl.BlockSpec((t, width), lambda p, qt, kt: (qt[p], 0)),
        scratch_shapes=[pltpu.VMEM((n_heads, 8, t), _F32), pltpu.VMEM((n_heads, 8, t), _F32),
                        pltpu.VMEM((width, t), _F32), pltpu.VMEM((t, t), _F32)],
    )
    return pl.pallas_call(
        functools.partial(_dsa_attn_kernel, n_heads=n_heads, hd=hd),
        grid_spec=grid_spec,
        out_shape=jax.ShapeDtypeStruct((T, width), _F32),
        compiler_params=_cparams(("arbitrary",)),
        name="dsa_attn",
    )(jnp.asarray(qi_tab), jnp.asarray(kj_tab), zb, zb, vT, maskT)


def _smp_index_kernel(pt_ref, iq_ref, iw_ref, ikn_ref, *rest, n_pages, page, Ts, topk, n_heads, hd, iw_lane0):
    page_refs = rest[:n_pages]
    mask_ref, lhs_sc, w_sc = rest[n_pages:]
    for h in range(n_heads):
        lhs_sc[h * Ts:(h + 1) * Ts, :] = iq_ref[:, h * hd:(h + 1) * hd]
        wcol = iw_ref[:, iw_lane0 + h:iw_lane0 + h + 1] * (n_heads ** -0.5)
        w_sc[h * Ts:(h + 1) * Ts, :] = jnp.broadcast_to(wcol, (Ts, page))
    lhs = lhs_sc[...].astype(_MXU)

    def keys_of(ik):
        s = _dot_nt(lhs, ik) * (hd ** -0.5)
        r = jnp.maximum(s, 0.0) * w_sc[...]
        out = r[0:Ts]
        for h in range(1, n_heads):
            out = out + r[h * Ts:(h + 1) * Ts]
        return _sort_key(out)

    cols = [keys_of(page_refs[j][0].astype(_MXU)) for j in range(n_pages)]
    ikn = jnp.concatenate([ikn_ref[...], jnp.zeros((page - Ts, hd), _F32)], axis=0).astype(_MXU)
    t_i = lax.broadcasted_iota(jnp.int32, (Ts, page), 0)
    j_i = lax.broadcasted_iota(jnp.int32, (Ts, page), 1)
    cols.append(jnp.where(j_i <= t_i, keys_of(ikn), _INT_MIN))
    keys = jnp.concatenate(cols, axis=1)

    def count_ge(cand):
        return jnp.sum(jnp.where(keys >= cand, 1.0, 0.0), axis=1, keepdims=True)

    thr = jnp.maximum(_kth_largest_key(count_ge, (Ts, 1), topk), _INT_MIN + 1)
    mask_ref[0] = jnp.where(keys >= thr, 1.0, 0.0)


def _smp_index(z, cache_idx, page_table, *, row0, NB, Ts, topk, lay):
    n_heads, hd = lay["idx_heads"], lay["idx_hd"]
    n_pages = page_table.shape[1]
    page = cache_idx.shape[1]
    width = n_heads * hd
    rb0 = row0 // Ts
    Lp = (n_pages + 1) * page

    def page_spec(j):
        return pl.BlockSpec((1, page, hd), lambda n, pt: (pt[n * n_pages + j], 0, 0))

    grid_spec = pltpu.PrefetchScalarGridSpec(
        num_scalar_prefetch=1,
        grid=(NB,),
        in_specs=[pl.BlockSpec((Ts, width), lambda n, pt: (rb0 + n, lay["iq"] // width)),
                  pl.BlockSpec((Ts, _LANES), lambda n, pt: (rb0 + n, lay["small"] // _LANES)),
                  pl.BlockSpec((Ts, hd), lambda n, pt: (rb0 + n, lay["ik"] // hd))]
                 + [page_spec(j) for j in range(n_pages)],
        out_specs=pl.BlockSpec((1, Ts, Lp), lambda n, pt: (n, 0, 0)),
        scratch_shapes=[pltpu.VMEM((n_heads * Ts, hd), _F32), pltpu.VMEM((n_heads * Ts, page), _F32)],
    )
    return pl.pallas_call(
        functools.partial(_smp_index_kernel, n_pages=n_pages, page=page, Ts=Ts, topk=topk, n_heads=n_heads,
                          hd=hd, iw_lane0=lay["iw_lane0"]),
        grid_spec=grid_spec,
        out_shape=jax.ShapeDtypeStruct((NB, Ts, Lp), _F32),
        compiler_params=_cparams(("parallel",)),
        name="smp_index_topk",
    )(page_table.reshape(-1), z, z, z, *([cache_idx] * n_pages))


def _smp_attn_kernel(pt_ref, q_ref, kn_ref, vn_ref, mask_ref, *rest, pg, n_groups, page, Ts, n_heads, hd):
    k_refs, v_refs = rest[:pg], rest[pg:2 * pg]
    o_ref, qp_sc, m_sc, l_sc, acc_sc = rest[2 * pg:]
    g = pl.program_id(1)
    zpad = jnp.zeros((Ts, hd), _F32)

    @pl.when(g == 0)
    def _():
        for h in range(n_heads):
            qp_sc[h] = jnp.concatenate([q_ref[:, h * hd:(h + 1) * hd], zpad], axis=0).astype(qp_sc.dtype)
        m_sc[...] = jnp.full_like(m_sc, _NEG)
        l_sc[...] = jnp.zeros_like(l_sc)
        acc_sc[...] = jnp.zeros_like(acc_sc)

    c = (hd ** -0.5) * _LOG2E

    def logits(get_k):
        return jnp.concatenate([_dot_nt(qp_sc[h], get_k(h))[:Ts] for h in range(n_heads)], axis=0)

    def step(u, get_v, mk_q):
        mk = jnp.concatenate([mk_q] * n_heads, axis=0) > 0.5
        u = jnp.where(mk, u * c, _NEG)
        m_prev = m_sc[...]
        m_new = jnp.maximum(m_prev, jnp.max(u, axis=1, keepdims=True))
        alpha = jnp.exp2(m_prev - m_new)
        pr = jnp.exp2(u - m_new)
        l_sc[...] = alpha * l_sc[...] + jnp.sum(pr, axis=1, keepdims=True)
        pv = []
        for h in range(n_heads):
            ph = jnp.concatenate([pr[h * Ts:(h + 1) * Ts], jnp.zeros((Ts, pr.shape[1]), _F32)], axis=0)
            pv.append(_dot(ph.astype(_MXU), get_v(h))[:Ts])
        acc_sc[...] = alpha * acc_sc[...] + jnp.concatenate(pv, axis=0)
        m_sc[...] = m_new

    us = [logits(lambda h, r=k_refs[j]: r[0, pl.ds(h, page, stride=n_heads), :].astype(_MXU)) for j in range(pg)]
    for j in range(pg):
        off = pl.multiple_of((g * pg + j) * page, page)
        step(us[j], lambda h, r=v_refs[j]: r[0, pl.ds(h, page, stride=n_heads), :].astype(_MXU),
             mask_ref[0, :, pl.ds(off, page)])

    @pl.when(g == n_groups - 1)
    def _():
        pad = jnp.zeros((page - Ts, hd), _F32)
        step(logits(lambda h: jnp.concatenate([kn_ref[:, h * hd:(h + 1) * hd], pad], axis=0).astype(_MXU)),
             lambda h: jnp.concatenate([vn_ref[:, h * hd:(h + 1) * hd], pad], axis=0).astype(_MXU),
             mask_ref[0, :, n_groups * pg * page:(n_groups * pg + 1) * page])
        for h in range(n_heads):
            rs = slice(h * Ts, (h + 1) * Ts)
            o_ref[:, h * hd:(h + 1) * hd] = acc_sc[rs, :] / l_sc[rs, :]


def _smp_attn(z, mask, cache_k, cache_v, page_table, *, row0, NB, Ts, lay):
    n_heads, hd = lay["dsa_heads"], lay["dsa_hd"]
    n_pages = page_table.shape[1]
    page = cache_k.shape[1] // n_heads
    width = n_heads * hd
    rb0 = row0 // Ts
    Lp = (n_pages + 1) * page
    pg = _pick(n_pages, (4, 2, 1))
    n_groups = n_pages // pg

    def page_spec(j):
        return pl.BlockSpec((1, page * n_heads, hd), lambda n, g, pt: (pt[n * n_pages + g * pg + j], 0, 0))

    grid_spec = pltpu.PrefetchScalarGridSpec(
        num_scalar_prefetch=1,
        grid=(NB, n_groups),
        in_specs=[pl.BlockSpec((Ts, width), lambda n, g, pt: (rb0 + n, lay["qd"] // width)),
                  pl.BlockSpec((Ts, width), lambda n, g, pt: (rb0 + n, lay["kd"] // width)),
                  pl.BlockSpec((Ts, width), lambda n, g, pt: (rb0 + n, lay["vd"] // width)),
                  pl.BlockSpec((1, Ts, Lp), lambda n, g, pt: (n, 0, 0))]
                 + [page_spec(j) for j in range(pg)] * 2,
        out_specs=pl.BlockSpec((Ts, width), lambda n, g, pt: (n, 0)),
        scratch_shapes=[pltpu.VMEM((n_heads, 2 * Ts, hd), _MXU), pltpu.VMEM((n_heads * Ts, 1), _F32),
                        pltpu.VMEM((n_heads * Ts, 1), _F32), pltpu.VMEM((n_heads * Ts, hd), _F32)],
    )
    return pl.pallas_call(
        functools.partial(_smp_attn_kernel, pg=pg, n_groups=n_groups, page=page, Ts=Ts, n_heads=n_heads, hd=hd),
        grid_spec=grid_spec,
        out_shape=jax.ShapeDtypeStruct((NB * Ts, width), _F32),
        compiler_params=_cparams(("parallel", "arbitrary")),
        name="smp_attn",
    )(page_table.reshape(-1), z, z, z, mask, *([cache_k] * pg), *([cache_v] * pg))


def _merge_kernel(a_ref, b_ref, wg_ref, wd_ref, ga_ref, gb_ref, o_ref, a_sc, b_sc):
    @pl.when(pl.program_id(1) == 0)
    def _():
        a_sc[...] = a_ref[...].astype(a_sc.dtype)
        b_sc[...] = b_ref[...].astype(b_sc.dtype)

    ya = _dot(a_sc[...], wg_ref[...])
    yb = _dot(b_sc[...], wd_ref[...])
    o_ref[...] = (jax.nn.sigmoid(ga_ref[...]) * ya + jax.nn.sigmoid(gb_ref[...]) * yb).astype(o_ref.dtype)


def _merge(a, b, wg, wd, z, *, lay):
    M, K = a.shape
    N = wg.shape[1]
    tm = _pick(M, (512, 256, 128))
    tn = _pick(N, (512, 256, 128))
    return pl.pallas_call(
        _merge_kernel,
        grid=(M // tm, N // tn),
        in_specs=[pl.BlockSpec((tm, K), lambda i, j: (i, 0)),
                  pl.BlockSpec((tm, K), lambda i, j: (i, 0)),
                  pl.BlockSpec((K, tn), lambda i, j: (0, j)),
                  pl.BlockSpec((K, tn), lambda i, j: (0, j)),
                  pl.BlockSpec((tm, tn), lambda i, j: (i, lay["ga"] // tn + j)),
                  pl.BlockSpec((tm, tn), lambda i, j: (i, lay["gb"] // tn + j))],
        out_specs=pl.BlockSpec((tm, tn), lambda i, j: (i, j)),
        out_shape=jax.ShapeDtypeStruct((M, N), _MXU),
        scratch_shapes=[pltpu.VMEM((tm, K), _MXU), pltpu.VMEM((tm, K), _MXU)],
        compiler_params=_cparams(("parallel", "arbitrary")),
        name="branch_merge",
    )(a, b, wg, wd, z, z)


def _out_proj_kernel(m_ref, w_ref, x_ref, g_ref, h_ref, hn_ref, *, tn, n_j):
    j = pl.program_id(1)
    off = pl.multiple_of(j * tn, tn)
    h_ref[:, pl.ds(off, tn)] = x_ref[...] + _dot(m_ref[...], w_ref[...])

    @pl.when(j == n_j - 1)
    def _():
        hn_ref[...] = _rms(h_ref[...], g_ref[...]).astype(hn_ref.dtype)


def _out_proj(m, w, x, gain):
    M, K = m.shape
    N = w.shape[1]
    tm = _pick(M, (512, 256, 128))
    tn = _pick(N, (512, 256, 128))
    return pl.pallas_call(
        functools.partial(_out_proj_kernel, tn=tn, n_j=N // tn),
        grid=(M // tm, N // tn),
        in_specs=[pl.BlockSpec((tm, K), lambda i, j: (i, 0)),
                  pl.BlockSpec((K, tn), lambda i, j: (0, j)),
                  pl.BlockSpec((tm, tn), lambda i, j: (i, j)),
                  pl.BlockSpec((1, N), lambda i, j: (0, 0))],
        out_specs=[pl.BlockSpec((tm, N), lambda i, j: (i, 0)),
                   pl.BlockSpec((tm, N), lambda i, j: (i, 0))],
        out_shape=[jax.ShapeDtypeStruct((M, N), _F32), jax.ShapeDtypeStruct((M, N), _MXU)],
        compiler_params=_cparams(("parallel", "arbitrary")),
        name="out_proj_norm",
    )(m, w, x, gain)


def _conv_gate(ua, ub, sa, sb, wca_ref, wcb_ref, bca_ref, bcb_ref):
    ca = sa[0] * wca_ref[0:1, :] + sa[1] * wca_ref[1:2, :] + ua * wca_ref[2:3, :] + bca_ref[...]
    cb = sb[0] * wcb_ref[0:1, :] + sb[1] * wcb_ref[1:2, :] + ub * wcb_ref[2:3, :] + bcb_ref[...]
    return jax.nn.gelu(ca, approximate=True) * cb


def _ffn_up_seq_kernel(hn_ref, wa_ref, wb_ref, wca_ref, wcb_ref, bca_ref, bcb_ref, g_ref, ta_ref, tb_ref,
                       ca_sc, cb_sc, *, tm):
    @pl.when(pl.program_id(1) == 0)
    def _():
        ca_sc[...] = jnp.zeros_like(ca_sc)
        cb_sc[...] = jnp.zeros_like(cb_sc)

    x = hn_ref[...]
    ua = _dot(x, wa_ref[...])
    ub = _dot(x, wb_ref[...])

    def shifted(u, carry_sc):
        ext = jnp.concatenate([carry_sc[...], u], axis=0)
        s1 = pltpu.roll(ext, 1, axis=0)[8:]
        s2 = pltpu.roll(ext, 2, axis=0)[8:]
        carry_sc[...] = u[tm - 8:]
        return s2, s1

    g_ref[...] = _conv_gate(ua, ub, shifted(ua, ca_sc), shifted(ub, cb_sc),
                            wca_ref, wcb_ref, bca_ref, bcb_ref).astype(g_ref.dtype)
    ta_ref[...] = ua[tm - 8:]
    tb_ref[...] = ub[tm - 8:]


def _ffn_up_seq(hn, w_up, w_conv, b_conv, *, T, ffp):
    K = hn.shape[1]
    tm = _pick(T, (512, 256, 128))
    tn = _pick(ffp, (512, 256, 128))
    nj = ffp // tn
    return pl.pallas_call(
        functools.partial(_ffn_up_seq_kernel, tm=tm),
        grid=(nj, T // tm),
        in_specs=[pl.BlockSpec((tm, K), lambda j, i: (i, 0)),
                  pl.BlockSpec((K, tn), lambda j, i: (0, j)),
                  pl.BlockSpec((K, tn), lambda j, i: (0, nj + j)),
                  pl.BlockSpec((3, tn), lambda j, i: (0, j)),
                  pl.BlockSpec((3, tn), lambda j, i: (0, nj + j)),
                  pl.BlockSpec((1, tn), lambda j, i: (0, j)),
                  pl.BlockSpec((1, tn), lambda j, i: (0, nj + j))],
        out_specs=[pl.BlockSpec((tm, tn), lambda j, i: (i, j)),
                   pl.BlockSpec((8, tn), lambda j, i: (0, j)),
                   pl.BlockSpec((8, tn), lambda j, i: (0, j))],
        out_shape=[jax.ShapeDtypeStruct((T, ffp), _MXU),
                   jax.ShapeDtypeStruct((8, ffp), _F32), jax.ShapeDtypeStruct((8, ffp), _F32)],
        scratch_shapes=[pltpu.VMEM((8, tn), _F32), pltpu.VMEM((8, tn), _F32)],
        compiler_params=_cparams(("parallel", "arbitrary")),
        name="ffn_up_conv_prompt",
    )(hn, w_up, w_up, w_conv, w_conv, b_conv, b_conv)


def _ffn_up_dec_kernel(hn_ref, wa_ref, wb_ref, wca_ref, wcb_ref, bca_ref, bcb_ref, e1a_ref, e1b_ref, e2a_ref,
                       e2b_ref, g_ref, ua_ref, ub_ref, *, Ts):
    x = hn_ref[...]
    ua = _dot(x, wa_ref[...])
    ub = _dot(x, wb_ref[...])
    r = lax.broadcasted_iota(jnp.int32, ua.shape, 0) % Ts

    def shifted(u, e1_ref, e2_ref):
        s1 = jnp.where(r == 0, e1_ref[...], pltpu.roll(u, 1, axis=0))
        s2 = jnp.where(r < 2, e2_ref[...], pltpu.roll(u, 2, axis=0))
        return s2, s1

    g_ref[...] = _conv_gate(ua, ub, shifted(ua, e1a_ref, e2a_ref), shifted(ub, e1b_ref, e2b_ref),
                            wca_ref, wcb_ref, bca_ref, bcb_ref).astype(g_ref.dtype)
    ua_ref[...] = ua
    ub_ref[...] = ub


def _ffn_up_dec(hn, w_up, w_conv, b_conv, e1, e2, *, row0, R, Ts, ffp):
    K = hn.shape[1]
    tn = _pick(ffp, (512, 256, 128))
    nj = ffp // tn
    rb0 = row0 // R
    return pl.pallas_call(
        functools.partial(_ffn_up_dec_kernel, Ts=Ts),
        grid=(nj,),
        in_specs=[pl.BlockSpec((R, K), lambda j: (rb0, 0)),
                  pl.BlockSpec((K, tn), lambda j: (0, j)),
                  pl.BlockSpec((K, tn), lambda j: (0, nj + j)),
                  pl.BlockSpec((3, tn), lambda j: (0, j)),
                  pl.BlockSpec((3, tn), lambda j: (0, nj + j)),
                  pl.BlockSpec((1, tn), lambda j: (0, j)),
                  pl.BlockSpec((1, tn), lambda j: (0, nj + j)),
                  pl.BlockSpec((R, tn), lambda j: (0, j)),
                  pl.BlockSpec((R, tn), lambda j: (0, nj + j)),
                  pl.BlockSpec((R, tn), lambda j: (0, j)),
                  pl.BlockSpec((R, tn), lambda j: (0, nj + j))],
        out_specs=[pl.BlockSpec((R, tn), lambda j: (0, j)),
                   pl.BlockSpec((R, tn), lambda j: (0, j)),
                   pl.BlockSpec((R, tn), lambda j: (0, j))],
        out_shape=[jax.ShapeDtypeStruct((R, ffp), _MXU),
                   jax.ShapeDtypeStruct((R, ffp), _F32), jax.ShapeDtypeStruct((R, ffp), _F32)],
        compiler_params=_cparams(("parallel",)),
        name="ffn_up_conv_sample",
    )(hn, w_up, w_up, w_conv, w_conv, b_conv, b_conv, e1, e1, e2, e2)


def _ffn_down_kernel(g_ref, w_ref, h_ref, gn_ref, y_ref, acc_sc, *, n_k):
    k = pl.program_id(1)

    @pl.when(k == 0)
    def _():
        acc_sc[...] = jnp.zeros_like(acc_sc)

    acc_sc[...] += _dot(g_ref[...], w_ref[...])

    @pl.when(k == n_k - 1)
    def _():
        y_ref[...] = _rms(h_ref[...] + acc_sc[...], gn_ref[...])


def _ffn_down(g, w, h, gain):
    M, K = g.shape
    N = w.shape[1]
    tm = _pick(M, (512, 256, 128))
    tk = _pick(K, (512, 256, 128))
    return pl.pallas_call(
        functools.partial(_ffn_down_kernel, n_k=K // tk),
        grid=(M // tm, K // tk),
        in_specs=[pl.BlockSpec((tm, tk), lambda i, k: (i, k)),
                  pl.BlockSpec((tk, N), lambda i, k: (k, 0)),
                  pl.BlockSpec((tm, N), lambda i, k: (i, 0)),
                  pl.BlockSpec((1, N), lambda i, k: (0, 0))],
        out_specs=pl.BlockSpec((tm, N), lambda i, k: (i, 0)),
        out_shape=jax.ShapeDtypeStruct((M, N), _F32),
        scratch_shapes=[pltpu.VMEM((tm, N), _F32)],
        compiler_params=_cparams(("parallel", "arbitrary")),
        name="ffn_down_norm",
    )(g, w, h, gain)


def _pad_cols(a, n):
    return a if a.shape[-1] == n else jnp.pad(a, [(0, 0)] * (a.ndim - 1) + [(0, n - a.shape[-1])])


def kernel(x_prompt, x_sample, cache_k, cache_v, cache_idx_k, state_gla, state_ffn_conv, page_table, norm_mix, w_in, w_a2, b_a, gla_norm, w_gla_proj, w_dsa_proj, w_out, norm_ffn, w_up, w_conv, b_conv, w_down, norm_final):
    depth = w_in.shape[0]
    assert depth == 1, "single-layer trunk"
    B, T, D = x_prompt.shape
    NB, Ts, _ = x_sample.shape
    assert B == 1 and Ts == 8
    _, _, GH, dk, dv = state_gla.shape
    rank = w_a2.shape[1]
    _, n_pool, page, DH, dhd = cache_k.shape
    ihd = cache_idx_k.shape[-1]
    n_mix = w_in.shape[-1]
    IH = (n_mix - (2 * GH * dk + 2 * GH * dv + rank + 3 * DH * dhd + ihd + 2 * D)) // (ihd + 1)
    ff = w_down.shape[1]
    n_pages = page_table.shape[1]
    past = n_pages * page
    R = NB * Ts
    assert page == _LANES and ihd == _LANES and dhd == _LANES and IH + rank <= _LANES and T % R == 0

    sizes = (GH * dk, GH * dk, GH * dv, GH * dv, rank, DH * dhd, DH * dhd, DH * dhd, IH * ihd, IH, ihd, 2 * D)
    o = np.concatenate([[0], np.cumsum(sizes)])
    assert o[-1] == n_mix
    lay = dict(gla_heads=GH, gla_dk=dk, gla_dv=dv, dsa_heads=DH, dsa_hd=dhd, idx_heads=IH, idx_hd=ihd,
               iw_lane0=rank)
    segs, pos = [], 0
    for name, lo, hi in (("qg", o[0], o[1]), ("kg", o[1], o[2]), ("vg", o[2], o[3]), ("rg", o[3], o[4]),
                         ("qd", o[5], o[6]), ("kd", o[6], o[7]), ("vd", o[7], o[8]), ("iq", o[8], o[9]),
                         ("ga", o[11], o[11] + D), ("gb", o[11] + D, o[12]), ("ik", o[10], o[11])):
        lay[name] = pos
        segs.append((int(lo), int(hi)))
        pos += int(hi - lo)
    lay["small"] = pos
    nz = -(-(pos + _LANES) // 512) * 512
    w0 = w_in[0].astype(_MXU)
    w_in_p = jnp.concatenate([w0[:, lo:hi] for lo, hi in segs]
                             + [w0[:, o[4]:o[5]], w0[:, o[9]:o[10]],
                                jnp.zeros((D, nz - pos - rank - IH), _MXU)], axis=1)
    wa_p = jnp.zeros((_LANES, GH * dk), _MXU).at[:rank].set(w_a2[0].astype(_MXU))

    x_all = jnp.concatenate([x_prompt[0], x_sample.reshape(R, D)], axis=0)
    z, zb = _norm_matmul(x_all, norm_mix, w_in_p)

    Cp = _pick(T, (128, 64, 32, 16, 8))
    og_p, S_p = _gla(z, wa_p, b_a, gla_norm, None, row0=0, n_seq=1, n_chunks=T // Cp, C=Cp, lay=lay)
    og_s, S_s = _gla(z, wa_p, b_a, gla_norm, state_gla[0], row0=T, n_seq=NB, n_chunks=1, C=Ts, lay=lay)

    iwT = z[:T, lay["small"] + rank:lay["small"] + rank + IH].T
    vT = zb[:T, lay["vd"]:lay["vd"] + DH * dhd].T
    maskT_p = _dsa_index(zb, iwT, T=T, topk=min(_TOPK_MAX, T // 4), lay=lay)
    od_p = _dsa_attn(zb, vT, maskT_p, T=T, lay=lay)
    mask_s = _smp_index(z, cache_idx_k[0], page_table, row0=T, NB=NB, Ts=Ts,
                        topk=min(_TOPK_MAX, (past + Ts) // 4), lay=lay)
    od_s = _smp_attn(z, mask_s, cache_k[0].reshape(n_pool, page * DH, dhd), cache_v[0].reshape(n_pool, page * DH, dhd),
                     page_table, row0=T, NB=NB, Ts=Ts, lay=lay)

    mixed = _merge(jnp.concatenate([og_p, og_s], axis=0), jnp.concatenate([od_p, od_s], axis=0),
                   w_gla_proj[0].astype(_MXU), w_dsa_proj[0].astype(_MXU), z, lay=lay)
    h, hn = _out_proj(mixed, w_out[0].astype(_MXU), x_all, norm_ffn)

    ffp = -(-ff // 512) * 512
    w_up_p = jnp.concatenate([_pad_cols(w_up[0, :, :ff].astype(_MXU), ffp),
                              _pad_cols(w_up[0, :, ff:].astype(_MXU), ffp)], axis=1)
    w_conv_p = jnp.concatenate([_pad_cols(w_conv[0, :, :ff], ffp), _pad_cols(w_conv[0, :, ff:], ffp)], axis=1)
    b_conv_p = jnp.concatenate([_pad_cols(b_conv[:, :ff], ffp), _pad_cols(b_conv[:, ff:], ffp)], axis=1)
    w_down_p = jnp.pad(w_down[0].astype(_MXU), ((0, ffp - ff), (0, 0)))
    g_p, ta, tb = _ffn_up_seq(hn, w_up_p, w_conv_p, b_conv_p, T=T, ffp=ffp)
    prev = state_ffn_conv[0]
    prev = jnp.concatenate([_pad_cols(prev[..., :ff], ffp), _pad_cols(prev[..., ff:], ffp)], axis=-1)
    e1 = jnp.zeros((NB, Ts, 2 * ffp), _F32).at[:, 0].set(prev[:, 1]).reshape(R, 2 * ffp)
    e2 = jnp.zeros((NB, Ts, 2 * ffp), _F32).at[:, 0].set(prev[:, 0]).at[:, 1].set(prev[:, 1]).reshape(R, 2 * ffp)
    g_s, ua_s, ub_s = _ffn_up_dec(hn, w_up_p, w_conv_p, b_conv_p, e1, e2, row0=T, R=R, Ts=Ts, ffp=ffp)
    y = _ffn_down(jnp.concatenate([g_p, g_s], axis=0), w_down_p, h, norm_final.reshape(1, D))

    def seg(name, width, rows):
        return z[rows, lay[name]:lay[name] + width]

    rp, rs = slice(0, T), slice(T, T + R)
    conv_p = jnp.concatenate([ta[6:8, :ff], tb[6:8, :ff]], axis=-1)
    conv_s = jnp.concatenate([ua_s.reshape(NB, Ts, ffp)[:, Ts - 2:, :ff], ub_s.reshape(NB, Ts, ffp)[:, Ts - 2:, :ff]],
                             axis=-1)
    return (y[rp].reshape(1, T, D), y[rs].reshape(NB, Ts, D),
            seg("kd", DH * dhd, rp).reshape(1, 1, T, DH, dhd), seg("vd", DH * dhd, rp).reshape(1, 1, T, DH, dhd),
            seg("ik", ihd, rp).reshape(1, 1, T, ihd),
            S_p.reshape(1, 1, GH, dk, dv), conv_p.reshape(1, 1, 2, 2 * ff),
            seg("kd", DH * dhd, rs).reshape(1, NB, Ts, DH, dhd), seg("vd", DH * dhd, rs).reshape(1, NB, Ts, DH, dhd),
            seg("ik", ihd, rs).reshape(1, NB, Ts, ihd),
            S_s.reshape(1, NB, GH, dk, dv), conv_s.reshape(1, NB, 2, 2 * ff))
```

```python
import functools

import jax
import jax.numpy as jnp
import numpy as np
from jax import lax
from jax.experimental import pallas as pl
from jax.experimental.pallas import tpu as pltpu

_F32 = jnp.float32
_MXU = jnp.bfloat16
_EPS = 1e-6
_GLA_TAU = 16.0
_GLA_SUB = 16
_GLA_PAD = 128
_TOPK_MAX = 256
_LANES = 128
_NEG = -1e30
_LOG2E = 1.4426950408889634
_INT_MIN = -(2 ** 31)
_VMEM_LIMIT = 56 * 1024 * 1024


def _cparams(sem, flags=None):
    return pltpu.CompilerParams(dimension_semantics=sem, vmem_limit_bytes=_VMEM_LIMIT, flags=flags)


def _pick(n, cands):
    for c in cands:
        if n % c == 0:
            return c
    raise ValueError(f"no tile in {cands} divides {n}")


def _dot(a, b):
    return jnp.dot(a, b, preferred_element_type=_F32)


def _dot_nt(a, b):
    return lax.dot_general(a, b, (((1,), (1,)), ((), ())), preferred_element_type=_F32)


def _rms(x, g):
    return x * lax.rsqrt(jnp.mean(x * x, axis=-1, keepdims=True) + _EPS) * g


def _sort_key(x):
    bits = lax.bitcast_convert_type(x, jnp.int32)
    return bits ^ ((bits >> 31) & jnp.int32(0x7FFFFFFF))


def _kth_largest_key(count_ge, shape, k):
    kf = float(k)
    lo = jnp.where(count_ge(jnp.zeros(shape, jnp.int32)) >= kf, 0, _INT_MIN).astype(jnp.int32)

    def step(it, lo):
        cand = lo + (jnp.int32(1) << (30 - it))
        return jnp.where(count_ge(cand) >= kf, cand, lo)

    return lax.fori_loop(0, 31, step, lo)


def _row_specs(tm, width, n_p, n_col=1):
    return [pl.BlockSpec((tm, width), lambda i, j: (jnp.minimum(i, n_p - 1),
                                                    jnp.where(i < n_p, j, n_col - 1) if n_col > 1 else 0)),
            pl.BlockSpec((tm, width), lambda i, j: (jnp.maximum(i - n_p, 0),
                                                    jnp.where(i >= n_p, j, 0) if n_col > 1 else 0))]


def _norm_matmul_kernel(x_ref, g_ref, w_ref, o_ref, ob_ref, og_ref, xn_sc, *, n_g):
    j = pl.program_id(1)

    @pl.when(j == 0)
    def _():
        xn_sc[...] = _rms(x_ref[...], g_ref[...]).astype(xn_sc.dtype)

    acc = _dot(xn_sc[...], w_ref[...])
    o_ref[...] = acc
    ob_ref[...] = acc.astype(ob_ref.dtype)

    @pl.when(j < n_g)
    def _():
        og_ref[0] = acc


def _norm_matmul(x, gain, w, *, n_g_cols):
    M, K = x.shape
    N = w.shape[1]
    tm = _pick(M, (1024, 512, 256, 128))
    tn = _pick(np.gcd(N, n_g_cols), (512, 256, 128))
    n_g = n_g_cols // tn
    return pl.pallas_call(
        functools.partial(_norm_matmul_kernel, n_g=n_g),
        grid=(M // tm, N // tn),
        in_specs=[pl.BlockSpec((tm, K), lambda i, j: (i, 0)),
                  pl.BlockSpec((1, K), lambda i, j: (0, 0)),
                  pl.BlockSpec((K, tn), lambda i, j: (0, j))],
        out_specs=[pl.BlockSpec((tm, tn), lambda i, j: (i, j)),
                   pl.BlockSpec((tm, tn), lambda i, j: (i, j)),
                   pl.BlockSpec((1, tm, tn), lambda i, j: (jnp.minimum(j, n_g - 1), i, 0))],
        out_shape=[jax.ShapeDtypeStruct((M, N), _F32), jax.ShapeDtypeStruct((M, N), _MXU),
                   jax.ShapeDtypeStruct((n_g, M, tn), _F32)],
        scratch_shapes=[pltpu.VMEM((tm, K), _MXU)],
        compiler_params=_cparams(("parallel", "arbitrary")),
        name="norm_in_proj",
    )(x, gain, w)


def _gla_kernel(*refs, C, G, has_init, dk, dv):
    q_ref, k_ref = refs[0], refs[1]
    v_refs, r_refs = refs[2:2 + G], refs[2 + G:2 + 2 * G]
    a_ref, wa_ref, ba_ref, gn_ref = refs[2 + 2 * G:6 + 2 * G]
    rest = refs[6 + 2 * G:]
    if has_init:
        s0_ref, o_ref, so_ref, S_sc, b_sc = rest
    else:
        o_ref, so_ref, S_sc, b_sc = rest
    P = _GLA_PAD
    c = pl.program_id(2)

    @pl.when(c == 0)
    def _():
        if has_init:
            S_sc[...] = s0_ref[0]
        else:
            S_sc[...] = jnp.zeros_like(S_sc)

    def padded(x):
        if C == P:
            return x
        return jnp.concatenate([x, jnp.zeros((P - C, x.shape[1]), x.dtype)], axis=0)

    heads = range(G)
    zz = _dot(padded(a_ref[...]).astype(_MXU), wa_ref[...]) + ba_ref[...]
    la = (jnp.minimum(zz, 0.0) - jnp.log(1.0 + jnp.exp(-jnp.abs(zz)))) / _GLA_TAU
    row = lax.broadcasted_iota(jnp.int32, la.shape, 0)
    if C < P:
        la = jnp.where(row < C, la, 0.0)
    b_all = la
    sh = 1
    while sh < P:
        b_all = b_all + jnp.where(row >= sh, pltpu.roll(b_all, sh, axis=0), 0.0)
        sh *= 2
    b_sc[...] = b_all
    q_all = padded(q_ref[0]) * (dk ** -0.5)
    k_all = padded(k_ref[0])
    b = [b_all[:, g * dk:(g + 1) * dk] for g in heads]
    q = [q_all[:, g * dk:(g + 1) * dk] for g in heads]
    k = [k_all[:, g * dk:(g + 1) * dk] for g in heads]
    v = [padded(v_refs[g][0]).astype(_MXU) for g in heads]
    S = [S_sc[g] for g in heads]

    o = [_dot((q[g] * jnp.exp(b[g])).astype(_MXU), S[g].astype(_MXU)) for g in heads]
    for g in heads:
        bT = b[g].T
        b_last = bT[:, P - 1:P]
        kdT = (k[g].T * jnp.exp(b_last - bT)).astype(_MXU)
        S_new = jnp.exp(b_last) * S[g] + _dot(kdT, v[g])
        S_sc[g] = S_new
        so_ref[0, g] = S_new
    n_sub = -(-C // _GLA_SUB)
    krow = lax.broadcasted_iota(jnp.int32, (P, dk), 0)
    a_rows = [[] for _ in heads]
    for i in range(n_sub):
        lo, hi = i * _GLA_SUB, (i + 1) * _GLA_SUB
        for g in heads:
            anchor = b_sc[pl.ds(lo - 1, 1), g * dk:(g + 1) * dk] if i > 0 else jnp.zeros((1, dk), _F32)
            qt = (q[g][lo:hi] * jnp.exp(b[g][lo:hi] - anchor)).astype(_MXU)
            kt = (k[g] * jnp.exp(jnp.where(krow < hi, anchor - b[g], 0.0))).astype(_MXU)
            a_i = _dot_nt(qt, kt)
            r_i = lax.broadcasted_iota(jnp.int32, a_i.shape, 0) + lo
            c_i = lax.broadcasted_iota(jnp.int32, a_i.shape, 1)
            a_rows[g].append(jnp.where(c_i <= r_i, a_i, 0.0))
    for g in heads:
        if n_sub * _GLA_SUB < P:
            a_rows[g].append(jnp.zeros((P - n_sub * _GLA_SUB, P), _F32))
        A = jnp.concatenate(a_rows[g], axis=0)
        og = (o[g] + _dot(A.astype(_MXU), v[g]))[:C]
        og = _rms(og, gn_ref[:, g * dv:(g + 1) * dv])
        r = r_refs[g][0]
        o_ref[:, g * dv:(g + 1) * dv] = (og * (r * jax.nn.sigmoid(r))).astype(o_ref.dtype)


def _gla(zg, a_seg, wa, ba, gn, s0, *, row0, n_seq, n_chunks, C, lay):
    H, dk, dv = lay["gla_heads"], lay["gla_dk"], lay["gla_dv"]
    tw = zg.shape[2]
    G = tw // dk
    assert G * dk == tw and dv == tw and H % G == 0
    rb0 = row0 // C
    tq, tk, tv, tr = (lay[n] // tw for n in ("qg", "kg", "vg", "rg"))

    def rows(s, c):
        return rb0 + s * n_chunks + c

    def tile(t0, per_pair, g):
        return pl.BlockSpec((1, C, tw), lambda s, hp, c: (t0 + per_pair * hp + g, rows(s, c), 0))

    in_specs = ([tile(tq, 1, 0), tile(tk, 1, 0)] + [tile(tv, G, g) for g in range(G)]
                + [tile(tr, G, g) for g in range(G)]
                + [pl.BlockSpec((C, _LANES), lambda s, hp, c: (rows(s, c), 0)),
                   pl.BlockSpec((_LANES, tw), lambda s, hp, c: (0, hp)),
                   pl.BlockSpec((1, tw), lambda s, hp, c: (0, hp)),
                   pl.BlockSpec((1, G * dv), lambda s, hp, c: (0, hp))])
    args = [zg] * (2 + 2 * G) + [a_seg, wa, ba, gn]
    if s0 is not None:
        in_specs.append(pl.BlockSpec((1, G, dk, dv), lambda s, hp, c: (s, hp, 0, 0)))
        args.append(s0)
    return pl.pallas_call(
        functools.partial(_gla_kernel, C=C, G=G, has_init=s0 is not None, dk=dk, dv=dv),
        grid=(n_seq, H // G, n_chunks),
        in_specs=in_specs,
        out_specs=[pl.BlockSpec((C, G * dv), lambda s, hp, c: (s * n_chunks + c, hp)),
                   pl.BlockSpec((1, G, dk, dv), lambda s, hp, c: (s, hp, 0, 0))],
        out_shape=[jax.ShapeDtypeStruct((n_seq * n_chunks * C, H * dv), _MXU if C % 16 == 0 else _F32),
                   jax.ShapeDtypeStruct((n_seq, H, dk, dv), _F32)],
        scratch_shapes=[pltpu.VMEM((G, dk, dv), _F32), pltpu.VMEM((_GLA_PAD, G * dk), _F32)],
        compiler_params=_cparams(("parallel", "parallel", "arbitrary")),
        name="gla",
    )(*args)


def _dsa_index_kernel(iq_ref, iwT_ref, ik_ref, maskT_ref, keys_sc, *, tq, tk, topk, n_heads, hd, n_ktiles):
    qi = pl.program_id(0)
    per_q = tq // tk
    nk = (qi + 1) * per_q
    wT = iwT_ref[...] * ((n_heads ** -0.5) * (hd ** -0.5))
    qpos = qi * tq + lax.broadcasted_iota(jnp.int32, (tk, tq), 1)
    krow = lax.broadcasted_iota(jnp.int32, (tk, tq), 0)

    def score_tile(kj, carry):
        off = pl.multiple_of(kj * tk, tk)
        ik = ik_ref[pl.ds(off, tk), :]
        acc = jnp.zeros((tk, tq), _F32)
        for h in range(n_heads):
            s = _dot_nt(ik, iq_ref[:, h * hd:(h + 1) * hd])
            acc = acc + jnp.maximum(s, 0.0) * wT[h:h + 1, :]
        keys_sc[pl.ds(off, tk), :] = jnp.where(krow + off <= qpos, _sort_key(acc), _INT_MIN)
        return carry

    lax.fori_loop(0, nk, score_tile, 0)

    def count_ge(cand):
        def body(kq, acc):
            off = pl.multiple_of(kq * tq, tq)
            hit = jnp.where(keys_sc[pl.ds(off, tq), :] >= cand, 1.0, 0.0)
            for r in range(tq // 8):
                acc = acc + hit[r * 8:(r + 1) * 8]
            return acc

        acc = lax.fori_loop(0, qi + 1, body, jnp.zeros((8, tq), _F32))
        return jnp.sum(acc, axis=0, keepdims=True)

    thr = jnp.maximum(_kth_largest_key(count_ge, (1, tq), topk), _INT_MIN + 1)

    def write_mask(kj, carry):
        off = pl.multiple_of(kj * tk, tk)
        maskT_ref[pl.ds(off, tk), :] = jnp.where(keys_sc[pl.ds(off, tk), :] >= thr, 1.0, 0.0).astype(maskT_ref.dtype)
        return carry

    lax.fori_loop(0, nk, write_mask, 0)

    def write_zero(kj, carry):
        off = pl.multiple_of(kj * tk, tk)
        maskT_ref[pl.ds(off, tk), :] = jnp.zeros((tk, tq), maskT_ref.dtype)
        return carry

    lax.fori_loop(nk, n_ktiles, write_zero, 0)


def _dsa_index(zb, iwT, *, T, topk, lay):
    n_heads, hd = lay["idx_heads"], lay["idx_hd"]
    tq = _pick(T, (256, 128))
    tk = 128
    width = n_heads * hd
    return pl.pallas_call(
        functools.partial(_dsa_index_kernel, tq=tq, tk=tk, topk=topk, n_heads=n_heads, hd=hd, n_ktiles=T // tk),
        grid=(T // tq,),
        in_specs=[pl.BlockSpec((tq, width), lambda i: (i, lay["iq"] // width)),
                  pl.BlockSpec((n_heads, tq), lambda i: (0, i)),
                  pl.BlockSpec((T, hd), lambda i: (0, lay["ik"] // hd))],
        out_specs=pl.BlockSpec((T, tq), lambda i: (0, i)),
        out_shape=jax.ShapeDtypeStruct((T, T), jnp.bfloat16),
        scratch_shapes=[pltpu.VMEM((T, tq), jnp.int32)],
        compiler_params=_cparams(("parallel",)),
        name="dsa_index_topk",
    )(zb, iwT, zb)


def _dsa_attn_kernel(qi_tab, kj_tab, q_ref, k_ref, vT_ref, maskT_ref, o_ref, m_sc, l_sc, acc_sc, bias_sc, *,
                     n_heads, hd):
    p = pl.program_id(0)
    qi = qi_tab[p]
    kj = kj_tab[p]

    @pl.when(kj == 0)
    def _():
        m_sc[...] = jnp.full_like(m_sc, _NEG)
        l_sc[...] = jnp.zeros_like(l_sc)
        acc_sc[...] = jnp.zeros_like(acc_sc)

    bias_sc[...] = jnp.where(maskT_ref[...].astype(_F32) > 0.5, 0.0, _NEG)
    c = (hd ** -0.5) * _LOG2E

    def logits(h):
        sl = slice(h * hd, (h + 1) * hd)
        return _dot_nt(k_ref[:, sl], q_ref[:, sl])

    ahead = 5
    pending = [logits(h) for h in range(min(ahead, n_heads))]
    for h in range(n_heads):
        sl = slice(h * hd, (h + 1) * hd)
        if h + ahead < n_heads:
            pending.append(logits(h + ahead))
        u = pending[h] * c + bias_sc[...]
        m_prev = m_sc[h, 0:1, :]
        m_new = jnp.maximum(m_prev, jnp.max(u, axis=0, keepdims=True))
        alpha = jnp.exp2(m_prev - m_new)
        pr = jnp.exp2(u - m_new)
        l_new = alpha * l_sc[h, 0:1, :] + jnp.sum(pr, axis=0, keepdims=True)
        acc_sc[sl, :] = alpha * acc_sc[sl, :] + _dot(vT_ref[sl, :], pr.astype(_MXU))
        m_sc[h] = jnp.broadcast_to(m_new, m_sc.shape[1:])
        l_sc[h] = jnp.broadcast_to(l_new, l_sc.shape[1:])

    @pl.when(kj == qi)
    def _():
        for h in range(n_heads):
            sl = slice(h * hd, (h + 1) * hd)
            o_ref[:, sl] = (acc_sc[sl, :] / l_sc[h, 0:1, :]).T.astype(o_ref.dtype)


def _dsa_attn(zb, vT, maskT, *, T, lay):
    n_heads, hd = lay["dsa_heads"], lay["dsa_hd"]
    width = n_heads * hd
    t = _pick(T, (256, 128))
    nq = T // t
    qi_tab = np.concatenate([np.full(i + 1, i, np.int32) for i in range(nq)])
    kj_tab = np.concatenate([np.arange(i + 1, dtype=np.int32) for i in range(nq)])
    grid_spec = pltpu.PrefetchScalarGridSpec(
        num_scalar_prefetch=2,
        grid=(len(qi_tab),),
        in_specs=[pl.BlockSpec((t, width), lambda p, qt, kt: (qt[p], lay["qd"] // width)),
                  pl.BlockSpec((t, width), lambda p, qt, kt: (kt[p], lay["kd"] // width)),
                  pl.BlockSpec((width, t), lambda p, qt, kt: (0, kt[p])),
                  pl.BlockSpec((t, t), lambda p, qt, kt: (kt[p], qt[p]))],
        out_specs=pl.BlockSpec((t, width), lambda p, qt, kt: (qt[p], 0)),
        scratch_shapes=[pltpu.VMEM((n_heads, 8, t), _F32), pltpu.VMEM((n_heads, 8, t), _F32),
                        pltpu.VMEM((width, t), _F32), pltpu.VMEM((t, t), _F32)],
    )
    return pl.pallas_call(
        functools.partial(_dsa_attn_kernel, n_heads=n_heads, hd=hd),
        grid_spec=grid_spec,
        out_shape=jax.ShapeDtypeStruct((T, width), _MXU),
        compiler_params=_cparams(("arbitrary",)),
        name="dsa_attn",
    )(jnp.asarray(qi_tab), jnp.asarray(kj_tab), zb, zb, vT, maskT)


def _smp_index_kernel(pt_ref, iq_ref, iw_ref, ikn_ref, *rest, n_pages, page, Ts, topk, n_heads, hd, iw_lane0):
    page_refs = rest[:n_pages]
    mask_ref, lhs_sc, w_sc = rest[n_pages:]
    for h in range(n_heads):
        lhs_sc[h * Ts:(h + 1) * Ts, :] = iq_ref[:, h * hd:(h + 1) * hd]
        wcol = iw_ref[:, iw_lane0 + h:iw_lane0 + h + 1] * (n_heads ** -0.5)
        w_sc[h * Ts:(h + 1) * Ts, :] = jnp.broadcast_to(wcol, (Ts, page))
    lhs = lhs_sc[...].astype(_MXU)

    def keys_of(ik):
        s = _dot_nt(lhs, ik) * (hd ** -0.5)
        r = jnp.maximum(s, 0.0) * w_sc[...]
        out = r[0:Ts]
        for h in range(1, n_heads):
            out = out + r[h * Ts:(h + 1) * Ts]
        return _sort_key(out)

    cols = [keys_of(page_refs[j][0].astype(_MXU)) for j in range(n_pages)]
    ikn = jnp.concatenate([ikn_ref[...], jnp.zeros((page - Ts, hd), _F32)], axis=0).astype(_MXU)
    t_i = lax.broadcasted_iota(jnp.int32, (Ts, page), 0)
    j_i = lax.broadcasted_iota(jnp.int32, (Ts, page), 1)
    cols.append(jnp.where(j_i <= t_i, keys_of(ikn), _INT_MIN))
    keys = jnp.concatenate(cols, axis=1)

    def count_ge(cand):
        return jnp.sum(jnp.where(keys >= cand, 1.0, 0.0), axis=1, keepdims=True)

    thr = jnp.maximum(_kth_largest_key(count_ge, (Ts, 1), topk), _INT_MIN + 1)
    mask_ref[0] = jnp.where(keys >= thr, 1.0, 0.0)


def _smp_index(z, cache_idx, page_table, *, row0, NB, Ts, topk, lay):
    n_heads, hd = lay["idx_heads"], lay["idx_hd"]
    n_pages = page_table.shape[1]
    page = cache_idx.shape[1]
    width = n_heads * hd
    rb0 = row0 // Ts
    Lp = (n_pages + 1) * page

    def page_spec(j):
        return pl.BlockSpec((1, page, hd), lambda n, pt: (pt[n * n_pages + j], 0, 0))

    grid_spec = pltpu.PrefetchScalarGridSpec(
        num_scalar_prefetch=1,
        grid=(NB,),
        in_specs=[pl.BlockSpec((Ts, width), lambda n, pt: (rb0 + n, lay["iq"] // width)),
                  pl.BlockSpec((Ts, _LANES), lambda n, pt: (rb0 + n, lay["small"] // _LANES)),
                  pl.BlockSpec((Ts, hd), lambda n, pt: (rb0 + n, lay["ik"] // hd))]
                 + [page_spec(j) for j in range(n_pages)],
        out_specs=pl.BlockSpec((1, Ts, Lp), lambda n, pt: (n, 0, 0)),
        scratch_shapes=[pltpu.VMEM((n_heads * Ts, hd), _F32), pltpu.VMEM((n_heads * Ts, page), _F32)],
    )
    return pl.pallas_call(
        functools.partial(_smp_index_kernel, n_pages=n_pages, page=page, Ts=Ts, topk=topk, n_heads=n_heads,
                          hd=hd, iw_lane0=lay["iw_lane0"]),
        grid_spec=grid_spec,
        out_shape=jax.ShapeDtypeStruct((NB, Ts, Lp), _F32),
        compiler_params=_cparams(("parallel",)),
        name="smp_index_topk",
    )(page_table.reshape(-1), z, z, z, *([cache_idx] * n_pages))


def _smp_attn_kernel(pt_ref, q_ref, kn_ref, vn_ref, mask_ref, *rest, pg, n_groups, page, Ts, n_heads, hd):
    k_refs, v_refs = rest[:pg], rest[pg:2 * pg]
    o_ref, qp_sc, m_sc, l_sc, acc_sc = rest[2 * pg:]
    g = pl.program_id(1)
    zpad = jnp.zeros((Ts, hd), _F32)

    @pl.when(g == 0)
    def _():
        for h in range(n_heads):
            qp_sc[h] = jnp.concatenate([q_ref[:, h * hd:(h + 1) * hd], zpad], axis=0).astype(qp_sc.dtype)
        m_sc[...] = jnp.full_like(m_sc, _NEG)
        l_sc[...] = jnp.zeros_like(l_sc)
        acc_sc[...] = jnp.zeros_like(acc_sc)

    c = (hd ** -0.5) * _LOG2E

    def logits(get_k):
        return jnp.concatenate([_dot_nt(qp_sc[h], get_k(h))[:Ts] for h in range(n_heads)], axis=0)

    def step(u, get_v, mk_q):
        mk = jnp.concatenate([mk_q] * n_heads, axis=0) > 0.5
        u = jnp.where(mk, u * c, _NEG)
        m_prev = m_sc[...]
        m_new = jnp.maximum(m_prev, jnp.max(u, axis=1, keepdims=True))
        alpha = jnp.exp2(m_prev - m_new)
        pr = jnp.exp2(u - m_new)
        l_sc[...] = alpha * l_sc[...] + jnp.sum(pr, axis=1, keepdims=True)
        pv = []
        for h in range(n_heads):
            ph = jnp.concatenate([pr[h * Ts:(h + 1) * Ts], jnp.zeros((Ts, pr.shape[1]), _F32)], axis=0)
            pv.append(_dot(ph.astype(_MXU), get_v(h))[:Ts])
        acc_sc[...] = alpha * acc_sc[...] + jnp.concatenate(pv, axis=0)
        m_sc[...] = m_new

    us = [logits(lambda h, r=k_refs[j]: r[0, pl.ds(h, page, stride=n_heads), :].astype(_MXU)) for j in range(pg)]
    for j in range(pg):
        off = pl.multiple_of((g * pg + j) * page, page)
        step(us[j], lambda h, r=v_refs[j]: r[0, pl.ds(h, page, stride=n_heads), :].astype(_MXU),
             mask_ref[0, :, pl.ds(off, page)])

    @pl.when(g == n_groups - 1)
    def _():
        pad = jnp.zeros((page - Ts, hd), _F32)
        step(logits(lambda h: jnp.concatenate([kn_ref[:, h * hd:(h + 1) * hd], pad], axis=0).astype(_MXU)),
             lambda h: jnp.concatenate([vn_ref[:, h * hd:(h + 1) * hd], pad], axis=0).astype(_MXU),
             mask_ref[0, :, n_groups * pg * page:(n_groups * pg + 1) * page])
        for h in range(n_heads):
            rs = slice(h * Ts, (h + 1) * Ts)
            o_ref[:, h * hd:(h + 1) * hd] = acc_sc[rs, :] / l_sc[rs, :]


def _smp_attn(z, mask, cache_k, cache_v, page_table, *, row0, NB, Ts, lay):
    n_heads, hd = lay["dsa_heads"], lay["dsa_hd"]
    n_pages = page_table.shape[1]
    page = cache_k.shape[1] // n_heads
    width = n_heads * hd
    rb0 = row0 // Ts
    Lp = (n_pages + 1) * page
    pg = _pick(n_pages, (4, 2, 1))
    n_groups = n_pages // pg

    def page_spec(j):
        return pl.BlockSpec((1, page * n_heads, hd), lambda n, g, pt: (pt[n * n_pages + g * pg + j], 0, 0))

    grid_spec = pltpu.PrefetchScalarGridSpec(
        num_scalar_prefetch=1,
        grid=(NB, n_groups),
        in_specs=[pl.BlockSpec((Ts, width), lambda n, g, pt: (rb0 + n, lay["qd"] // width)),
                  pl.BlockSpec((Ts, width), lambda n, g, pt: (rb0 + n, lay["kd"] // width)),
                  pl.BlockSpec((Ts, width), lambda n, g, pt: (rb0 + n, lay["vd"] // width)),
                  pl.BlockSpec((1, Ts, Lp), lambda n, g, pt: (n, 0, 0))]
                 + [page_spec(j) for j in range(pg)] * 2,
        out_specs=pl.BlockSpec((Ts, width), lambda n, g, pt: (n, 0)),
        scratch_shapes=[pltpu.VMEM((n_heads, 2 * Ts, hd), _MXU), pltpu.VMEM((n_heads * Ts, 1), _F32),
                        pltpu.VMEM((n_heads * Ts, 1), _F32), pltpu.VMEM((n_heads * Ts, hd), _F32)],
    )
    return pl.pallas_call(
        functools.partial(_smp_attn_kernel, pg=pg, n_groups=n_groups, page=page, Ts=Ts, n_heads=n_heads, hd=hd),
        grid_spec=grid_spec,
        out_shape=jax.ShapeDtypeStruct((NB * Ts, width), _F32),
        compiler_params=_cparams(("parallel", "arbitrary")),
        name="smp_attn",
    )(page_table.reshape(-1), z, z, z, mask, *([cache_k] * pg), *([cache_v] * pg))


def _merge_kernel(ap_ref, as_ref, bp_ref, bs_ref, wg_ref, wd_ref, ga_ref, gb_ref, o_ref, a_sc, b_sc, *, n_p):
    i, j = pl.program_id(0), pl.program_id(1)

    @pl.when((j == 0) & (i < n_p))
    def _():
        a_sc[...] = ap_ref[...].astype(a_sc.dtype)
        b_sc[...] = bp_ref[...].astype(b_sc.dtype)

    @pl.when((j == 0) & (i >= n_p))
    def _():
        a_sc[...] = as_ref[...].astype(a_sc.dtype)
        b_sc[...] = bs_ref[...].astype(b_sc.dtype)

    ya = _dot(a_sc[...], wg_ref[...])
    yb = _dot(b_sc[...], wd_ref[...])
    o_ref[...] = (jax.nn.sigmoid(ga_ref[...]) * ya + jax.nn.sigmoid(gb_ref[...]) * yb).astype(o_ref.dtype)


def _merge(a_p, a_s, b_p, b_s, wg, wd, z, *, lay):
    T, K = a_p.shape
    R = a_s.shape[0]
    N = wg.shape[1]
    tm = _pick(np.gcd(T, R), (512, 256, 128))
    tn = _pick(N, (512, 256, 128))
    n_p = T // tm
    return pl.pallas_call(
        functools.partial(_merge_kernel, n_p=n_p),
        grid=((T + R) // tm, N // tn),
        in_specs=_row_specs(tm, K, n_p) + _row_specs(tm, K, n_p)
                 + [pl.BlockSpec((K, tn), lambda i, j: (0, j)),
                    pl.BlockSpec((K, tn), lambda i, j: (0, j)),
                    pl.BlockSpec((tm, tn), lambda i, j: (i, lay["ga"] // tn + j)),
                    pl.BlockSpec((tm, tn), lambda i, j: (i, lay["gb"] // tn + j))],
        out_specs=pl.BlockSpec((tm, tn), lambda i, j: (i, j)),
        out_shape=jax.ShapeDtypeStruct((T + R, N), _MXU),
        scratch_shapes=[pltpu.VMEM((tm, K), _MXU), pltpu.VMEM((tm, K), _MXU)],
        compiler_params=_cparams(("parallel", "arbitrary")),
        name="branch_merge",
    )(a_p, a_s, b_p, b_s, wg, wd, z, z)


def _out_proj_kernel(m_ref, w_ref, x_ref, g_ref, h_ref, hn_ref, *, tn, n_j):
    j = pl.program_id(1)
    off = pl.multiple_of(j * tn, tn)
    h_ref[:, pl.ds(off, tn)] = x_ref[...] + _dot(m_ref[...], w_ref[...])

    @pl.when(j == n_j - 1)
    def _():
        hn_ref[...] = _rms(h_ref[...], g_ref[...]).astype(hn_ref.dtype)


def _out_proj(m, w, x, gain):
    M, K = m.shape
    N = w.shape[1]
    tm = _pick(M, (512, 256, 128))
    tn = _pick(N, (512, 256, 128))
    return pl.pallas_call(
        functools.partial(_out_proj_kernel, tn=tn, n_j=N // tn),
        grid=(M // tm, N // tn),
        in_specs=[pl.BlockSpec((tm, K), lambda i, j: (i, 0)),
                  pl.BlockSpec((K, tn), lambda i, j: (0, j)),
                  pl.BlockSpec((tm, tn), lambda i, j: (i, j)),
                  pl.BlockSpec((1, N), lambda i, j: (0, 0))],
        out_specs=[pl.BlockSpec((tm, N), lambda i, j: (i, 0)),
                   pl.BlockSpec((tm, N), lambda i, j: (i, 0))],
        out_shape=[jax.ShapeDtypeStruct((M, N), _F32), jax.ShapeDtypeStruct((M, N), _MXU)],
        compiler_params=_cparams(("parallel", "arbitrary")),
        name="out_proj_norm",
    )(m, w, x, gain)


def _conv_gate(ua, ub, sa, sb, wca_ref, wcb_ref, bca_ref, bcb_ref):
    ca = sa[0] * wca_ref[0:1, :] + sa[1] * wca_ref[1:2, :] + ua * wca_ref[2:3, :] + bca_ref[...]
    cb = sb[0] * wcb_ref[0:1, :] + sb[1] * wcb_ref[1:2, :] + ub * wcb_ref[2:3, :] + bcb_ref[...]
    return jax.nn.gelu(ca, approximate=True) * cb


def _ffn_up_seq_kernel(hn_ref, wa_ref, wb_ref, wca_ref, wcb_ref, bca_ref, bcb_ref, g_ref, ta_ref, tb_ref,
                       ca_sc, cb_sc, *, tm):
    @pl.when(pl.program_id(1) == 0)
    def _():
        ca_sc[...] = jnp.zeros_like(ca_sc)
        cb_sc[...] = jnp.zeros_like(cb_sc)

    x = hn_ref[...]
    ua = _dot(x, wa_ref[...])
    ub = _dot(x, wb_ref[...])

    def shifted(u, carry_sc):
        ext = jnp.concatenate([carry_sc[...], u], axis=0)
        s1 = pltpu.roll(ext, 1, axis=0)[8:]
        s2 = pltpu.roll(ext, 2, axis=0)[8:]
        carry_sc[...] = u[tm - 8:]
        return s2, s1

    g_ref[...] = _conv_gate(ua, ub, shifted(ua, ca_sc), shifted(ub, cb_sc),
                            wca_ref, wcb_ref, bca_ref, bcb_ref).astype(g_ref.dtype)
    ta_ref[...] = ua[tm - 8:]
    tb_ref[...] = ub[tm - 8:]


def _ffn_up_seq(hn, w_up, w_conv, b_conv, *, T, ffp):
    K = hn.shape[1]
    tm = _pick(T, (512, 256, 128))
    tn = _pick(ffp, (512, 256, 128))
    nj = ffp // tn
    return pl.pallas_call(
        functools.partial(_ffn_up_seq_kernel, tm=tm),
        grid=(nj, T // tm),
        in_specs=[pl.BlockSpec((tm, K), lambda j, i: (i, 0)),
                  pl.BlockSpec((K, tn), lambda j, i: (0, j)),
                  pl.BlockSpec((K, tn), lambda j, i: (0, nj + j)),
                  pl.BlockSpec((3, tn), lambda j, i: (0, j)),
                  pl.BlockSpec((3, tn), lambda j, i: (0, nj + j)),
                  pl.BlockSpec((1, tn), lambda j, i: (0, j)),
                  pl.BlockSpec((1, tn), lambda j, i: (0, nj + j))],
        out_specs=[pl.BlockSpec((tm, tn), lambda j, i: (i, j)),
                   pl.BlockSpec((8, tn), lambda j, i: (0, j)),
                   pl.BlockSpec((8, tn), lambda j, i: (0, j))],
        out_shape=[jax.ShapeDtypeStruct((T, ffp), _MXU),
                   jax.ShapeDtypeStruct((8, ffp), _F32), jax.ShapeDtypeStruct((8, ffp), _F32)],
        scratch_shapes=[pltpu.VMEM((8, tn), _F32), pltpu.VMEM((8, tn), _F32)],
        compiler_params=_cparams(("parallel", "arbitrary")),
        name="ffn_up_conv_prompt",
    )(hn, w_up, w_up, w_conv, w_conv, b_conv, b_conv)


def _ffn_up_dec_kernel(hn_ref, wa_ref, wb_ref, wca_ref, wcb_ref, bca_ref, bcb_ref, e1a_ref, e1b_ref, e2a_ref,
                       e2b_ref, g_ref, ua_ref, ub_ref, *, Ts):
    x = hn_ref[...]
    ua = _dot(x, wa_ref[...])
    ub = _dot(x, wb_ref[...])
    r = lax.broadcasted_iota(jnp.int32, ua.shape, 0) % Ts

    def shifted(u, e1_ref, e2_ref):
        s1 = jnp.where(r == 0, e1_ref[...], pltpu.roll(u, 1, axis=0))
        s2 = jnp.where(r < 2, e2_ref[...], pltpu.roll(u, 2, axis=0))
        return s2, s1

    g_ref[...] = _conv_gate(ua, ub, shifted(ua, e1a_ref, e2a_ref), shifted(ub, e1b_ref, e2b_ref),
                            wca_ref, wcb_ref, bca_ref, bcb_ref).astype(g_ref.dtype)
    ua_ref[...] = ua
    ub_ref[...] = ub


def _ffn_up_dec(hn, w_up, w_conv, b_conv, e1, e2, *, row0, R, Ts, ffp):
    K = hn.shape[1]
    tn = _pick(ffp, (512, 256, 128))
    nj = ffp // tn
    rb0 = row0 // R
    return pl.pallas_call(
        functools.partial(_ffn_up_dec_kernel, Ts=Ts),
        grid=(nj,),
        in_specs=[pl.BlockSpec((R, K), lambda j: (rb0, 0)),
                  pl.BlockSpec((K, tn), lambda j: (0, j)),
                  pl.BlockSpec((K, tn), lambda j: (0, nj + j)),
                  pl.BlockSpec((3, tn), lambda j: (0, j)),
                  pl.BlockSpec((3, tn), lambda j: (0, nj + j)),
                  pl.BlockSpec((1, tn), lambda j: (0, j)),
                  pl.BlockSpec((1, tn), lambda j: (0, nj + j)),
                  pl.BlockSpec((R, tn), lambda j: (0, j)),
                  pl.BlockSpec((R, tn), lambda j: (0, nj + j)),
                  pl.BlockSpec((R, tn), lambda j: (0, j)),
                  pl.BlockSpec((R, tn), lambda j: (0, nj + j))],
        out_specs=[pl.BlockSpec((R, tn), lambda j: (0, j)),
                   pl.BlockSpec((R, tn), lambda j: (0, j)),
                   pl.BlockSpec((R, tn), lambda j: (0, j))],
        out_shape=[jax.ShapeDtypeStruct((R, ffp), _MXU),
                   jax.ShapeDtypeStruct((R, ffp), _F32), jax.ShapeDtypeStruct((R, ffp), _F32)],
        compiler_params=_cparams(("parallel",)),
        name="ffn_up_conv_sample",
    )(hn, w_up, w_up, w_conv, w_conv, b_conv, b_conv, e1, e1, e2, e2)


def _ffn_down_kernel(gp_ref, gs_ref, w_ref, h_ref, gn_ref, y_ref, acc_sc, *, n_k, n_p):
    i, k = pl.program_id(0), pl.program_id(1)

    @pl.when(k == 0)
    def _():
        acc_sc[...] = jnp.zeros_like(acc_sc)

    @pl.when(i < n_p)
    def _():
        acc_sc[...] += _dot(gp_ref[...], w_ref[...])

    @pl.when(i >= n_p)
    def _():
        acc_sc[...] += _dot(gs_ref[...], w_ref[...])

    @pl.when(k == n_k - 1)
    def _():
        y_ref[...] = _rms(h_ref[...] + acc_sc[...], gn_ref[...])


def _ffn_down(g_p, g_s, w, h, gain):
    T, K = g_p.shape
    R = g_s.shape[0]
    N = w.shape[1]
    tm = _pick(np.gcd(T, R), (512, 256, 128))
    tk = _pick(K, (512, 256, 128))
    n_p = T // tm
    return pl.pallas_call(
        functools.partial(_ffn_down_kernel, n_k=K // tk, n_p=n_p),
        grid=((T + R) // tm, K // tk),
        in_specs=_row_specs(tm, tk, n_p, n_col=K // tk)
                 + [pl.BlockSpec((tk, N), lambda i, k: (k, 0)),
                    pl.BlockSpec((tm, N), lambda i, k: (i, 0)),
                    pl.BlockSpec((1, N), lambda i, k: (0, 0))],
        out_specs=pl.BlockSpec((tm, N), lambda i, k: (i, 0)),
        out_shape=jax.ShapeDtypeStruct((T + R, N), _F32),
        scratch_shapes=[pltpu.VMEM((tm, N), _F32)],
        compiler_params=_cparams(("parallel", "arbitrary")),
        name="ffn_down_norm",
    )(g_p, g_s, w, h, gain)


def _pad_cols(a, n):
    return a if a.shape[-1] == n else jnp.pad(a, [(0, 0)] * (a.ndim - 1) + [(0, n - a.shape[-1])])


def kernel(x_prompt, x_sample, cache_k, cache_v, cache_idx_k, state_gla, state_ffn_conv, page_table, norm_mix, w_in, w_a2, b_a, gla_norm, w_gla_proj, w_dsa_proj, w_out, norm_ffn, w_up, w_conv, b_conv, w_down, norm_final):
    depth = w_in.shape[0]
    assert depth == 1, "single-layer trunk"
    B, T, D = x_prompt.shape
    NB, Ts, _ = x_sample.shape
    assert B == 1 and Ts == 8
    _, _, GH, dk, dv = state_gla.shape
    rank = w_a2.shape[1]
    _, n_pool, page, DH, dhd = cache_k.shape
    ihd = cache_idx_k.shape[-1]
    n_mix = w_in.shape[-1]
    IH = (n_mix - (2 * GH * dk + 2 * GH * dv + rank + 3 * DH * dhd + ihd + 2 * D)) // (ihd + 1)
    ff = w_down.shape[1]
    n_pages = page_table.shape[1]
    past = n_pages * page
    R = NB * Ts
    assert page == _LANES and ihd == _LANES and dhd == _LANES and IH + rank <= _LANES and T % R == 0

    sizes = (GH * dk, GH * dk, GH * dv, GH * dv, rank, DH * dhd, DH * dhd, DH * dhd, IH * ihd, IH, ihd, 2 * D)
    o = np.concatenate([[0], np.cumsum(sizes)])
    assert o[-1] == n_mix
    lay = dict(gla_heads=GH, gla_dk=dk, gla_dv=dv, dsa_heads=DH, dsa_hd=dhd, idx_heads=IH, idx_hd=ihd,
               iw_lane0=rank)
    segs, pos = [], 0
    for name, lo, hi in (("qg", o[0], o[1]), ("kg", o[1], o[2]), ("vg", o[2], o[3]), ("rg", o[3], o[4]),
                         ("qd", o[5], o[6]), ("kd", o[6], o[7]), ("vd", o[7], o[8]), ("iq", o[8], o[9]),
                         ("ga", o[11], o[11] + D), ("gb", o[11] + D, o[12]), ("ik", o[10], o[11])):
        lay[name] = pos
        segs.append((int(lo), int(hi)))
        pos += int(hi - lo)
    lay["small"] = pos
    nz = -(-(pos + _LANES) // 512) * 512
    w0 = w_in[0].astype(_MXU)
    w_in_p = jnp.concatenate([w0[:, lo:hi] for lo, hi in segs]
                             + [w0[:, o[4]:o[5]], w0[:, o[9]:o[10]],
                                jnp.zeros((D, nz - pos - rank - IH), _MXU)], axis=1)
    wa_p = jnp.zeros((_LANES, GH * dk), _MXU).at[:rank].set(w_a2[0].astype(_MXU))

    x_all = jnp.concatenate([x_prompt[0], x_sample.reshape(R, D)], axis=0)
    z, zb, zg = _norm_matmul(x_all, norm_mix, w_in_p, n_g_cols=lay["qd"])

    Cp = _pick(T, (128, 64, 32, 16, 8))
    a_seg = z[:, lay["small"]:lay["small"] + _LANES]
    og_p, S_p = _gla(zg, a_seg, wa_p, b_a, gla_norm, None, row0=0, n_seq=1, n_chunks=T // Cp, C=Cp, lay=lay)
    og_s, S_s = _gla(zg, a_seg, wa_p, b_a, gla_norm, state_gla[0], row0=T, n_seq=NB, n_chunks=1, C=Ts, lay=lay)

    iwT = z[:T, lay["small"] + rank:lay["small"] + rank + IH].T
    vT = zb[:T, lay["vd"]:lay["vd"] + DH * dhd].T
    maskT_p = _dsa_index(zb, iwT, T=T, topk=min(_TOPK_MAX, T // 4), lay=lay)
    od_p = _dsa_attn(zb, vT, maskT_p, T=T, lay=lay)
    mask_s = _smp_index(z, cache_idx_k[0], page_table, row0=T, NB=NB, Ts=Ts,
                        topk=min(_TOPK_MAX, (past + Ts) // 4), lay=lay)
    od_s = _smp_attn(z, mask_s, cache_k[0].reshape(n_pool, page * DH, dhd), cache_v[0].reshape(n_pool, page * DH, dhd),
                     page_table, row0=T, NB=NB, Ts=Ts, lay=lay)

    mixed = _merge(og_p, og_s, od_p, od_s, w_gla_proj[0].astype(_MXU), w_dsa_proj[0].astype(_MXU), z, lay=lay)
    h, hn = _out_proj(mixed, w_out[0].astype(_MXU), x_all, norm_ffn)

    ffp = -(-ff // 512) * 512
    w_up_p = jnp.concatenate([_pad_cols(w_up[0, :, :ff].astype(_MXU), ffp),
                              _pad_cols(w_up[0, :, ff:].astype(_MXU), ffp)], axis=1)
    w_conv_p = jnp.concatenate([_pad_cols(w_conv[0, :, :ff], ffp), _pad_cols(w_conv[0, :, ff:], ffp)], axis=1)
    b_conv_p = jnp.concatenate([_pad_cols(b_conv[:, :ff], ffp), _pad_cols(b_conv[:, ff:], ffp)], axis=1)
    w_down_p = jnp.pad(w_down[0].astype(_MXU), ((0, ffp - ff), (0, 0)))
    g_p, ta, tb = _ffn_up_seq(hn, w_up_p, w_conv_p, b_conv_p, T=T, ffp=ffp)
    prev = state_ffn_conv[0]
    prev = jnp.concatenate([_pad_cols(prev[..., :ff], ffp), _pad_cols(prev[..., ff:], ffp)], axis=-1)
    e1 = jnp.pad(prev[:, 1:2], ((0, 0), (0, Ts - 1), (0, 0))).reshape(R, 2 * ffp)
    e2 = jnp.pad(prev, ((0, 0), (0, Ts - 2), (0, 0))).reshape(R, 2 * ffp)
    g_s, ua_s, ub_s = _ffn_up_dec(hn, w_up_p, w_conv_p, b_conv_p, e1, e2, row0=T, R=R, Ts=Ts, ffp=ffp)
    y = _ffn_down(g_p, g_s, w_down_p, h, norm_final.reshape(1, D))

    def seg(name, width, rows):
        return z[rows, lay[name]:lay[name] + width]

    rp, rs = slice(0, T), slice(T, T + R)
    conv_p = jnp.concatenate([ta[6:8, :ff], tb[6:8, :ff]], axis=-1)
    conv_s = jnp.concatenate([ua_s.reshape(NB, Ts, ffp)[:, Ts - 2:, :ff], ub_s.reshape(NB, Ts, ffp)[:, Ts - 2:, :ff]],
                             axis=-1)
    return (y[rp].reshape(1, T, D), y[rs].reshape(NB, Ts, D),
            seg("kd", DH * dhd, rp).reshape(1, 1, T, DH, dhd), seg("vd", DH * dhd, rp).reshape(1, 1, T, DH, dhd),
            seg("ik", ihd, rp).reshape(1, 1, T, ihd),
            S_p.reshape(1, 1, GH, dk, dv), conv_p.reshape(1, 1, 2, 2 * ff),
            seg("kd", DH * dhd, rs).reshape(1, NB, Ts, DH, dhd), seg("vd", DH * dhd, rs).reshape(1, NB, Ts, DH, dhd),
            seg("ik", ihd, rs).reshape(1, NB, Ts, ihd),
            S_s.reshape(1, NB, GH, dk, dv), conv_s.reshape(1, NB, 2, 2 * ff))
```

```python
import functools

import jax
import jax.numpy as jnp
import numpy as np
from jax import lax
from jax.experimental import pallas as pl
from jax.experimental.pallas import tpu as pltpu

_F32 = jnp.float32
_MXU = jnp.bfloat16
_EPS = 1e-6
_GLA_TAU = 16.0
_GLA_SUB = 16
_GLA_PAD = 128
_TOPK_MAX = 256
_LANES = 128
_NEG = -1e30
_LOG2E = 1.4426950408889634
_INT_MIN = -(2 ** 31)
_VMEM_LIMIT = 56 * 1024 * 1024


def _cparams(sem, flags=None):
    return pltpu.CompilerParams(dimension_semantics=sem, vmem_limit_bytes=_VMEM_LIMIT, flags=flags)


def _pick(n, cands):
    for c in cands:
        if n % c == 0:
            return c
    raise ValueError(f"no tile in {cands} divides {n}")


def _dot(a, b):
    return jnp.dot(a, b, preferred_element_type=_F32)


def _dot_nt(a, b):
    return lax.dot_general(a, b, (((1,), (1,)), ((), ())), preferred_element_type=_F32)


def _rms(x, g):
    return x * lax.rsqrt(jnp.mean(x * x, axis=-1, keepdims=True) + _EPS) * g


def _sort_key(x):
    bits = lax.bitcast_convert_type(x, jnp.int32)
    return bits ^ ((bits >> 31) & jnp.int32(0x7FFFFFFF))


def _kth_largest_key(count_ge, shape, k):
    kf = float(k)
    lo = jnp.where(count_ge(jnp.zeros(shape, jnp.int32)) >= kf, 0, _INT_MIN).astype(jnp.int32)

    def step(it, lo):
        cand = lo + (jnp.int32(1) << (30 - it))
        return jnp.where(count_ge(cand) >= kf, cand, lo)

    return lax.fori_loop(0, 31, step, lo)


def _row_specs(tm, width, n_p, n_col=1):
    return [pl.BlockSpec((tm, width), lambda i, j: (jnp.minimum(i, n_p - 1),
                                                    jnp.where(i < n_p, j, n_col - 1) if n_col > 1 else 0)),
            pl.BlockSpec((tm, width), lambda i, j: (jnp.maximum(i - n_p, 0),
                                                    jnp.where(i >= n_p, j, 0) if n_col > 1 else 0))]


def _norm_matmul_kernel(x_ref, g_ref, w_ref, o_ref, ob_ref, og_ref, xn_sc, *, n_g):
    j = pl.program_id(1)

    @pl.when(j == 0)
    def _():
        xn_sc[...] = _rms(x_ref[...], g_ref[...]).astype(xn_sc.dtype)

    acc = _dot(xn_sc[...], w_ref[...])
    o_ref[...] = acc
    ob_ref[...] = acc.astype(ob_ref.dtype)

    @pl.when(j < n_g)
    def _():
        og_ref[0] = acc


def _norm_matmul(x, gain, w, *, n_g_cols):
    M, K = x.shape
    N = w.shape[1]
    tm = _pick(M, (1024, 512, 256, 128))
    tn = _pick(np.gcd(N, n_g_cols), (512, 256, 128))
    n_g = n_g_cols // tn
    return pl.pallas_call(
        functools.partial(_norm_matmul_kernel, n_g=n_g),
        grid=(M // tm, N // tn),
        in_specs=[pl.BlockSpec((tm, K), lambda i, j: (i, 0)),
                  pl.BlockSpec((1, K), lambda i, j: (0, 0)),
                  pl.BlockSpec((K, tn), lambda i, j: (0, j))],
        out_specs=[pl.BlockSpec((tm, tn), lambda i, j: (i, j)),
                   pl.BlockSpec((tm, tn), lambda i, j: (i, j)),
                   pl.BlockSpec((1, tm, tn), lambda i, j: (jnp.minimum(j, n_g - 1), i, 0))],
        out_shape=[jax.ShapeDtypeStruct((M, N), _F32), jax.ShapeDtypeStruct((M, N), _MXU),
                   jax.ShapeDtypeStruct((n_g, M, tn), _F32)],
        scratch_shapes=[pltpu.VMEM((tm, K), _MXU)],
        compiler_params=_cparams(("parallel", "arbitrary")),
        name="norm_in_proj",
    )(x, gain, w)


def _gla_kernel(*refs, C, NS, G, hpt, has_init, dk, dv):
    q_ref, k_ref, v_ref, r_ref, a_ref, wa_ref, ba_ref, gn_ref = refs[:8]
    if has_init:
        s0_ref, o_ref, so_ref, S_sc, b_sc = refs[8:]
    else:
        o_ref, so_ref, S_sc, b_sc = refs[8:]
    P = _GLA_PAD
    c = pl.program_id(2)

    @pl.when(c == 0)
    def _():
        if has_init:
            S_sc[...] = s0_ref[...]
        else:
            S_sc[...] = jnp.zeros_like(S_sc)

    def padded(x):
        if C == P:
            return x
        return jnp.concatenate([x, jnp.zeros((P - C, x.shape[1]), x.dtype)], axis=0)

    units = [(i, g) for i in range(NS) for g in range(G)]
    row = lax.broadcasted_iota(jnp.int32, (P, G * dk), 0)
    b_seq = []
    for i in range(NS):
        rs = slice(i * C, (i + 1) * C)
        zz = _dot(padded(a_ref[rs, :]).astype(_MXU), wa_ref[...]) + ba_ref[...]
        la = (jnp.minimum(zz, 0.0) - jnp.log(1.0 + jnp.exp(-jnp.abs(zz)))) / _GLA_TAU
        if C < P:
            la = jnp.where(row < C, la, 0.0)
        bi = la
        sh = 1
        while sh < P:
            bi = bi + jnp.where(row >= sh, pltpu.roll(bi, sh, axis=0), 0.0)
            sh *= 2
        b_sc[i] = bi
        b_seq.append(bi)

    def head_cols(ref, i, g):
        return ref[g // hpt, i * C:(i + 1) * C, (g % hpt) * dk:(g % hpt + 1) * dk]

    b = {u: b_seq[u[0]][:, u[1] * dk:(u[1] + 1) * dk] for u in units}
    q = {u: padded(head_cols(q_ref, *u)) * (dk ** -0.5) for u in units}
    k = {u: padded(head_cols(k_ref, *u)) for u in units}
    v = {u: padded(v_ref[u[1], u[0] * C:(u[0] + 1) * C, :]).astype(_MXU) for u in units}
    S = {u: S_sc[u[0], u[1]] for u in units}

    o = {u: _dot((q[u] * jnp.exp(b[u])).astype(_MXU), S[u].astype(_MXU)) for u in units}
    for u in units:
        bT = b[u].T
        b_last = bT[:, P - 1:P]
        kdT = (k[u].T * jnp.exp(b_last - bT)).astype(_MXU)
        S_new = jnp.exp(b_last) * S[u] + _dot(kdT, v[u])
        S_sc[u[0], u[1]] = S_new
        so_ref[u[0], u[1]] = S_new
    n_sub = -(-C // _GLA_SUB)
    krow = lax.broadcasted_iota(jnp.int32, (P, dk), 0)
    a_rows = {u: [] for u in units}
    for j in range(n_sub):
        lo, hi = j * _GLA_SUB, (j + 1) * _GLA_SUB
        for u in units:
            i, g = u
            anchor = b_sc[i, pl.ds(lo - 1, 1), g * dk:(g + 1) * dk] if j > 0 else jnp.zeros((1, dk), _F32)
            qt = (q[u][lo:hi] * jnp.exp(b[u][lo:hi] - anchor)).astype(_MXU)
            kt = (k[u] * jnp.exp(jnp.where(krow < hi, anchor - b[u], 0.0))).astype(_MXU)
            a_j = _dot_nt(qt, kt)
            r_j = lax.broadcasted_iota(jnp.int32, a_j.shape, 0) + lo
            c_j = lax.broadcasted_iota(jnp.int32, a_j.shape, 1)
            a_rows[u].append(jnp.where(c_j <= r_j, a_j, 0.0))
    for u in units:
        i, g = u
        if n_sub * _GLA_SUB < P:
            a_rows[u].append(jnp.zeros((P - n_sub * _GLA_SUB, P), _F32))
        A = jnp.concatenate(a_rows[u], axis=0)
        og = (o[u] + _dot(A.astype(_MXU), v[u]))[:C]
        og = _rms(og, gn_ref[:, g * dv:(g + 1) * dv])
        r = r_ref[g, i * C:(i + 1) * C, :]
        o_ref[i * C:(i + 1) * C, g * dv:(g + 1) * dv] = (og * (r * jax.nn.sigmoid(r))).astype(o_ref.dtype)


def _gla(zg, a_seg, wa, ba, gn, s0, *, row0, n_seq, n_chunks, C, lay):
    H, dk, dv = lay["gla_heads"], lay["gla_dk"], lay["gla_dv"]
    tw = zg.shape[2]
    hpt = tw // dk
    nt = H // hpt
    NS = _pick(n_seq, (4, 2, 1)) if n_chunks == 1 else 1
    rows_blk = NS * C
    tq, tk, tv, tr = (lay[n] // tw for n in ("qg", "kg", "vg", "rg"))
    assert hpt * dk == tw and dv == tw and H % hpt == 0
    assert tq % nt == 0 and tk % nt == 0 and tv % H == 0 and tr % H == 0 and row0 % rows_blk == 0
    rb0 = row0 // rows_blk

    def rows(s, c):
        return rb0 + s * n_chunks + c

    in_specs = [pl.BlockSpec((nt, rows_blk, tw), lambda s, hp, c: (tq // nt, rows(s, c), 0)),
                pl.BlockSpec((nt, rows_blk, tw), lambda s, hp, c: (tk // nt, rows(s, c), 0)),
                pl.BlockSpec((H, rows_blk, tw), lambda s, hp, c: (tv // H, rows(s, c), 0)),
                pl.BlockSpec((H, rows_blk, tw), lambda s, hp, c: (tr // H, rows(s, c), 0)),
                pl.BlockSpec((rows_blk, _LANES), lambda s, hp, c: (rows(s, c), 0)),
                pl.BlockSpec((_LANES, H * dk), lambda s, hp, c: (0, 0)),
                pl.BlockSpec((1, H * dk), lambda s, hp, c: (0, 0)),
                pl.BlockSpec((1, H * dv), lambda s, hp, c: (0, 0))]
    args = [zg, zg, zg, zg, a_seg, wa, ba, gn]
    if s0 is not None:
        in_specs.append(pl.BlockSpec((NS, H, dk, dv), lambda s, hp, c: (s, 0, 0, 0)))
        args.append(s0)
    return pl.pallas_call(
        functools.partial(_gla_kernel, C=C, NS=NS, G=H, hpt=hpt, has_init=s0 is not None, dk=dk, dv=dv),
        grid=(n_seq // NS, 1, n_chunks),
        in_specs=in_specs,
        out_specs=[pl.BlockSpec((rows_blk, H * dv), lambda s, hp, c: (s * n_chunks + c, 0)),
                   pl.BlockSpec((NS, H, dk, dv), lambda s, hp, c: (s, 0, 0, 0))],
        out_shape=[jax.ShapeDtypeStruct((n_seq * n_chunks * C, H * dv), _MXU if rows_blk % 16 == 0 else _F32),
                   jax.ShapeDtypeStruct((n_seq, H, dk, dv), _F32)],
        scratch_shapes=[pltpu.VMEM((NS, H, dk, dv), _F32), pltpu.VMEM((NS, _GLA_PAD, H * dk), _F32)],
        compiler_params=_cparams(("parallel", "parallel", "arbitrary")),
        name="gla",
    )(*args)


def _dsa_index_kernel(iq_ref, iwT_ref, ik_ref, maskT_ref, keys_sc, *, tq, tk, topk, n_heads, hd, n_ktiles):
    qi = pl.program_id(0)
    per_q = tq // tk
    nk = (qi + 1) * per_q
    wT = iwT_ref[...] * ((n_heads ** -0.5) * (hd ** -0.5))
    qpos = qi * tq + lax.broadcasted_iota(jnp.int32, (tk, tq), 1)
    krow = lax.broadcasted_iota(jnp.int32, (tk, tq), 0)

    def score_tile(kj, carry):
        off = pl.multiple_of(kj * tk, tk)
        ik = ik_ref[pl.ds(off, tk), :]
        acc = jnp.zeros((tk, tq), _F32)
        for h in range(n_heads):
            s = _dot_nt(ik, iq_ref[:, h * hd:(h + 1) * hd])
            acc = acc + jnp.maximum(s, 0.0) * wT[h:h + 1, :]
        keys_sc[pl.ds(off, tk), :] = jnp.where(krow + off <= qpos, _sort_key(acc), _INT_MIN)
        return carry

    lax.fori_loop(0, nk, score_tile, 0)

    def count_ge(cand):
        def body(kq, acc):
            off = pl.multiple_of(kq * tq, tq)
            hit = jnp.where(keys_sc[pl.ds(off, tq), :] >= cand, 1.0, 0.0)
            for r in range(tq // 8):
                acc = acc + hit[r * 8:(r + 1) * 8]
            return acc

        acc = lax.fori_loop(0, qi + 1, body, jnp.zeros((8, tq), _F32))
        return jnp.sum(acc, axis=0, keepdims=True)

    thr = jnp.maximum(_kth_largest_key(count_ge, (1, tq), topk), _INT_MIN + 1)

    def write_mask(kj, carry):
        off = pl.multiple_of(kj * tk, tk)
        maskT_ref[pl.ds(off, tk), :] = jnp.where(keys_sc[pl.ds(off, tk), :] >= thr, 1.0, 0.0).astype(maskT_ref.dtype)
        return carry

    lax.fori_loop(0, nk, write_mask, 0)

    def write_zero(kj, carry):
        off = pl.multiple_of(kj * tk, tk)
        maskT_ref[pl.ds(off, tk), :] = jnp.zeros((tk, tq), maskT_ref.dtype)
        return carry

    lax.fori_loop(nk, n_ktiles, write_zero, 0)


def _dsa_index(zb, iwT, *, T, topk, lay):
    n_heads, hd = lay["idx_heads"], lay["idx_hd"]
    tq = _pick(T, (256, 128))
    tk = 128
    width = n_heads * hd
    return pl.pallas_call(
        functools.partial(_dsa_index_kernel, tq=tq, tk=tk, topk=topk, n_heads=n_heads, hd=hd, n_ktiles=T // tk),
        grid=(T // tq,),
        in_specs=[pl.BlockSpec((tq, width), lambda i: (i, lay["iq"] // width)),
                  pl.BlockSpec((n_heads, tq), lambda i: (0, i)),
                  pl.BlockSpec((T, hd), lambda i: (0, lay["ik"] // hd))],
        out_specs=pl.BlockSpec((T, tq), lambda i: (0, i)),
        out_shape=jax.ShapeDtypeStruct((T, T), jnp.bfloat16),
        scratch_shapes=[pltpu.VMEM((T, tq), jnp.int32)],
        compiler_params=_cparams(("parallel",)),
        name="dsa_index_topk",
    )(zb, iwT, zb)


def _dsa_attn_kernel(qi_tab, kj_tab, q_ref, k_ref, vT_ref, maskT_ref, o_ref, m_sc, l_sc, acc_sc, bias_sc, *,
                     n_heads, hd):
    p = pl.program_id(0)
    qi = qi_tab[p]
    kj = kj_tab[p]

    @pl.when(kj == 0)
    def _():
        m_sc[...] = jnp.full_like(m_sc, _NEG)
        l_sc[...] = jnp.zeros_like(l_sc)
        acc_sc[...] = jnp.zeros_like(acc_sc)

    bias_sc[...] = jnp.where(maskT_ref[...].astype(_F32) > 0.5, 0.0, _NEG)
    c = (hd ** -0.5) * _LOG2E

    def logits(h):
        sl = slice(h * hd, (h + 1) * hd)
        return _dot_nt(k_ref[:, sl], q_ref[:, sl])

    ahead = 5
    pending = [logits(h) for h in range(min(ahead, n_heads))]
    for h in range(n_heads):
        sl = slice(h * hd, (h + 1) * hd)
        if h + ahead < n_heads:
            pending.append(logits(h + ahead))
        u = pending[h] * c + bias_sc[...]
        m_prev = m_sc[h, 0:1, :]
        m_new = jnp.maximum(m_prev, jnp.max(u, axis=0, keepdims=True))
        alpha = jnp.exp2(m_prev - m_new)
        pr = jnp.exp2(u - m_new)
        l_new = alpha * l_sc[h, 0:1, :] + jnp.sum(pr, axis=0, keepdims=True)
        acc_sc[sl, :] = alpha * acc_sc[sl, :] + _dot(vT_ref[sl, :], pr.astype(_MXU))
        m_sc[h] = jnp.broadcast_to(m_new, m_sc.shape[1:])
        l_sc[h] = jnp.broadcast_to(l_new, l_sc.shape[1:])

    @pl.when(kj == qi)
    def _():
        for h in range(n_heads):
            sl = slice(h * hd, (h + 1) * hd)
            o_ref[:, sl] = (acc_sc[sl, :] / l_sc[h, 0:1, :]).T.astype(o_ref.dtype)


def _dsa_attn(zb, vT, maskT, *, T, lay):
    n_heads, hd = lay["dsa_heads"], lay["dsa_hd"]
    width = n_heads * hd
    t = _pick(T, (256, 128))
    nq = T // t
    qi_tab = np.concatenate([np.full(i + 1, i, np.int32) for i in range(nq)])
    kj_tab = np.concatenate([np.arange(i + 1, dtype=np.int32) for i in range(nq)])
    grid_spec = pltpu.PrefetchScalarGridSpec(
        num_scalar_prefetch=2,
        grid=(len(qi_tab),),
        in_specs=[pl.BlockSpec((t, width), lambda p, qt, kt: (qt[p], lay["qd"] // width)),
                  pl.BlockSpec((t, width), lambda p, qt, kt: (kt[p], lay["kd"] // width)),
                  pl.BlockSpec((width, t), lambda p, qt, kt: (0, kt[p])),
                  pl.BlockSpec((t, t), lambda p, qt, kt: (kt[p], qt[p]))],
        out_specs=pl.BlockSpec((t, width), lambda p, qt, kt: (qt[p], 0)),
        scratch_shapes=[pltpu.VMEM((n_heads, 8, t), _F32), pltpu.VMEM((n_heads, 8, t), _F32),
                        pltpu.VMEM((width, t), _F32), pltpu.VMEM((t, t), _F32)],
    )
    return pl.pallas_call(
        functools.partial(_dsa_attn_kernel, n_heads=n_heads, hd=hd),
        grid_spec=grid_spec,
        out_shape=jax.ShapeDtypeStruct((T, width), _MXU),
        compiler_params=_cparams(("arbitrary",)),
        name="dsa_attn",
    )(jnp.asarray(qi_tab), jnp.asarray(kj_tab), zb, zb, vT, maskT)


def _smp_topk_mask(iq_ref, iw_ref, ikn_ref, page_refs, lhs_sc, w_sc, *, page, Ts, topk, n_heads, hd, iw_lane0):
    n_pages = len(page_refs)
    for h in range(n_heads):
        lhs_sc[h * Ts:(h + 1) * Ts, :] = iq_ref[:, h * hd:(h + 1) * hd]
        wcol = iw_ref[:, iw_lane0 + h:iw_lane0 + h + 1] * (n_heads ** -0.5)
        w_sc[h * Ts:(h + 1) * Ts, :] = jnp.broadcast_to(wcol, (Ts, page))
    lhs = lhs_sc[...].astype(_MXU)

    def keys_of(ik):
        s = _dot_nt(lhs, ik) * (hd ** -0.5)
        r = jnp.maximum(s, 0.0) * w_sc[...]
        out = r[0:Ts]
        for h in range(1, n_heads):
            out = out + r[h * Ts:(h + 1) * Ts]
        return _sort_key(out)

    cols = [keys_of(page_refs[j][0].astype(_MXU)) for j in range(n_pages)]
    ikn = jnp.concatenate([ikn_ref[...], jnp.zeros((page - Ts, hd), _F32)], axis=0).astype(_MXU)
    t_i = lax.broadcasted_iota(jnp.int32, (Ts, page), 0)
    j_i = lax.broadcasted_iota(jnp.int32, (Ts, page), 1)
    cols.append(jnp.where(j_i <= t_i, keys_of(ikn), _INT_MIN))
    keys = jnp.concatenate(cols, axis=1)

    def count_ge(cand):
        return jnp.sum(jnp.where(keys >= cand, 1.0, 0.0), axis=1, keepdims=True)

    thr = jnp.maximum(_kth_largest_key(count_ge, (Ts, 1), topk), _INT_MIN + 1)
    return jnp.where(keys >= thr, 1.0, 0.0)


def _smp_attn_kernel(pt_ref, q_ref, kn_ref, vn_ref, iq_ref, iw_ref, ikn_ref, *rest, n_pages, pg, n_groups, page, Ts,
                     n_heads, hd, topk, idx_heads, idx_hd, iw_lane0):
    idx_refs, k_refs, v_refs = rest[:n_pages], rest[n_pages:n_pages + pg], rest[n_pages + pg:n_pages + 2 * pg]
    o_ref, qp_sc, m_sc, l_sc, acc_sc, mask_sc, lhs_sc, w_sc = rest[n_pages + 2 * pg:]
    g = pl.program_id(1)
    zpad = jnp.zeros((Ts, hd), _F32)

    @pl.when(g == 0)
    def _():
        mask_sc[...] = _smp_topk_mask(iq_ref, iw_ref, ikn_ref, idx_refs, lhs_sc, w_sc, page=page, Ts=Ts, topk=topk,
                                      n_heads=idx_heads, hd=idx_hd, iw_lane0=iw_lane0)
        for h in range(n_heads):
            qp_sc[h] = jnp.concatenate([q_ref[:, h * hd:(h + 1) * hd], zpad], axis=0).astype(qp_sc.dtype)
        m_sc[...] = jnp.full_like(m_sc, _NEG)
        l_sc[...] = jnp.zeros_like(l_sc)
        acc_sc[...] = jnp.zeros_like(acc_sc)

    c = (hd ** -0.5) * _LOG2E

    def logits(get_k):
        return jnp.concatenate([_dot_nt(qp_sc[h], get_k(h))[:Ts] for h in range(n_heads)], axis=0)

    def step(u, get_v, mk_q):
        mk = jnp.concatenate([mk_q] * n_heads, axis=0) > 0.5
        u = jnp.where(mk, u * c, _NEG)
        m_prev = m_sc[...]
        m_new = jnp.maximum(m_prev, jnp.max(u, axis=1, keepdims=True))
        alpha = jnp.exp2(m_prev - m_new)
        pr = jnp.exp2(u - m_new)
        l_sc[...] = alpha * l_sc[...] + jnp.sum(pr, axis=1, keepdims=True)
        pv = []
        for h in range(n_heads):
            ph = jnp.concatenate([pr[h * Ts:(h + 1) * Ts], jnp.zeros((Ts, pr.shape[1]), _F32)], axis=0)
            pv.append(_dot(ph.astype(_MXU), get_v(h))[:Ts])
        acc_sc[...] = alpha * acc_sc[...] + jnp.concatenate(pv, axis=0)
        m_sc[...] = m_new

    us = [logits(lambda h, r=k_refs[j]: r[0, pl.ds(h, page, stride=n_heads), :].astype(_MXU)) for j in range(pg)]
    for j in range(pg):
        off = pl.multiple_of((g * pg + j) * page, page)
        step(us[j], lambda h, r=v_refs[j]: r[0, pl.ds(h, page, stride=n_heads), :].astype(_MXU),
             mask_sc[:, pl.ds(off, page)])

    @pl.when(g == n_groups - 1)
    def _():
        pad = jnp.zeros((page - Ts, hd), _F32)
        step(logits(lambda h: jnp.concatenate([kn_ref[:, h * hd:(h + 1) * hd], pad], axis=0).astype(_MXU)),
             lambda h: jnp.concatenate([vn_ref[:, h * hd:(h + 1) * hd], pad], axis=0).astype(_MXU),
             mask_sc[:, n_groups * pg * page:(n_groups * pg + 1) * page])
        for h in range(n_heads):
            rs = slice(h * Ts, (h + 1) * Ts)
            o_ref[:, h * hd:(h + 1) * hd] = acc_sc[rs, :] / l_sc[rs, :]


def _smp_attn(z, cache_idx, cache_k, cache_v, page_table, *, row0, NB, Ts, topk, lay):
    n_heads, hd = lay["dsa_heads"], lay["dsa_hd"]
    ih, ihd = lay["idx_heads"], lay["idx_hd"]
    n_pages = page_table.shape[1]
    page = cache_idx.shape[1]
    width = n_heads * hd
    rb0 = row0 // Ts
    Lp = (n_pages + 1) * page
    pg = _pick(n_pages, (8, 4, 2, 1))
    n_groups = n_pages // pg

    def rows(col):
        return lambda n, g, pt: (rb0 + n, col)

    def idx_spec(j):
        return pl.BlockSpec((1, page, ihd), lambda n, g, pt: (pt[n * n_pages + j], 0, 0))

    def page_spec(j):
        return pl.BlockSpec((1, page * n_heads, hd), lambda n, g, pt: (pt[n * n_pages + g * pg + j], 0, 0))

    grid_spec = pltpu.PrefetchScalarGridSpec(
        num_scalar_prefetch=1,
        grid=(NB, n_groups),
        in_specs=[pl.BlockSpec((Ts, width), rows(lay["qd"] // width)),
                  pl.BlockSpec((Ts, width), rows(lay["kd"] // width)),
                  pl.BlockSpec((Ts, width), rows(lay["vd"] // width)),
                  pl.BlockSpec((Ts, ih * ihd), rows(lay["iq"] // (ih * ihd))),
                  pl.BlockSpec((Ts, _LANES), rows(lay["small"] // _LANES)),
                  pl.BlockSpec((Ts, ihd), rows(lay["ik"] // ihd))]
                 + [idx_spec(j) for j in range(n_pages)] + [page_spec(j) for j in range(pg)] * 2,
        out_specs=pl.BlockSpec((Ts, width), lambda n, g, pt: (n, 0)),
        scratch_shapes=[pltpu.VMEM((n_heads, 2 * Ts, hd), _MXU), pltpu.VMEM((n_heads * Ts, 1), _F32),
                        pltpu.VMEM((n_heads * Ts, 1), _F32), pltpu.VMEM((n_heads * Ts, hd), _F32),
                        pltpu.VMEM((Ts, Lp), _F32), pltpu.VMEM((ih * Ts, ihd), _F32),
                        pltpu.VMEM((ih * Ts, page), _F32)],
    )
    return pl.pallas_call(
        functools.partial(_smp_attn_kernel, n_pages=n_pages, pg=pg, n_groups=n_groups, page=page, Ts=Ts,
                          n_heads=n_heads, hd=hd, topk=topk, idx_heads=ih, idx_hd=ihd, iw_lane0=lay["iw_lane0"]),
        grid_spec=grid_spec,
        out_shape=jax.ShapeDtypeStruct((NB * Ts, width), _F32),
        compiler_params=_cparams(("parallel", "arbitrary")),
        name="smp_dsa",
    )(page_table.reshape(-1), z, z, z, z, z, z, *([cache_idx] * n_pages), *([cache_k] * pg), *([cache_v] * pg))


def _merge_kernel(ap_ref, as_ref, bp_ref, bs_ref, wg_ref, wd_ref, ga_ref, gb_ref, o_ref, a_sc, b_sc, *, n_p):
    i, j = pl.program_id(0), pl.program_id(1)

    @pl.when((j == 0) & (i < n_p))
    def _():
        a_sc[...] = ap_ref[...].astype(a_sc.dtype)
        b_sc[...] = bp_ref[...].astype(b_sc.dtype)

    @pl.when((j == 0) & (i >= n_p))
    def _():
        a_sc[...] = as_ref[...].astype(a_sc.dtype)
        b_sc[...] = bs_ref[...].astype(b_sc.dtype)

    ya = _dot(a_sc[...], wg_ref[...])
    yb = _dot(b_sc[...], wd_ref[...])
    o_ref[...] = (jax.nn.sigmoid(ga_ref[...]) * ya + jax.nn.sigmoid(gb_ref[...]) * yb).astype(o_ref.dtype)


def _merge(a_p, a_s, b_p, b_s, wg, wd, z, *, lay):
    T, K = a_p.shape
    R = a_s.shape[0]
    N = wg.shape[1]
    tm = _pick(np.gcd(T, R), (512, 256, 128))
    tn = _pick(N, (512, 256, 128))
    n_p = T // tm
    return pl.pallas_call(
        functools.partial(_merge_kernel, n_p=n_p),
        grid=((T + R) // tm, N // tn),
        in_specs=_row_specs(tm, K, n_p) + _row_specs(tm, K, n_p)
                 + [pl.BlockSpec((K, tn), lambda i, j: (0, j)),
                    pl.BlockSpec((K, tn), lambda i, j: (0, j)),
                    pl.BlockSpec((tm, tn), lambda i, j: (i, lay["ga"] // tn + j)),
                    pl.BlockSpec((tm, tn), lambda i, j: (i, lay["gb"] // tn + j))],
        out_specs=pl.BlockSpec((tm, tn), lambda i, j: (i, j)),
        out_shape=jax.ShapeDtypeStruct((T + R, N), _MXU),
        scratch_shapes=[pltpu.VMEM((tm, K), _MXU), pltpu.VMEM((tm, K), _MXU)],
        compiler_params=_cparams(("parallel", "arbitrary")),
        name="branch_merge",
    )(a_p, a_s, b_p, b_s, wg, wd, z, z)


def _out_proj_kernel(m_ref, w_ref, x_ref, g_ref, h_ref, hn_ref, *, tn, n_j):
    j = pl.program_id(1)
    off = pl.multiple_of(j * tn, tn)
    h_ref[:, pl.ds(off, tn)] = x_ref[...] + _dot(m_ref[...], w_ref[...])

    @pl.when(j == n_j - 1)
    def _():
        hn_ref[...] = _rms(h_ref[...], g_ref[...]).astype(hn_ref.dtype)


def _out_proj(m, w, x, gain):
    M, K = m.shape
    N = w.shape[1]
    tm = _pick(M, (512, 256, 128))
    tn = _pick(N, (512, 256, 128))
    return pl.pallas_call(
        functools.partial(_out_proj_kernel, tn=tn, n_j=N // tn),
        grid=(M // tm, N // tn),
        in_specs=[pl.BlockSpec((tm, K), lambda i, j: (i, 0)),
                  pl.BlockSpec((K, tn), lambda i, j: (0, j)),
                  pl.BlockSpec((tm, tn), lambda i, j: (i, j)),
                  pl.BlockSpec((1, N), lambda i, j: (0, 0))],
        out_specs=[pl.BlockSpec((tm, N), lambda i, j: (i, 0)),
                   pl.BlockSpec((tm, N), lambda i, j: (i, 0))],
        out_shape=[jax.ShapeDtypeStruct((M, N), _F32), jax.ShapeDtypeStruct((M, N), _MXU)],
        compiler_params=_cparams(("parallel", "arbitrary")),
        name="out_proj_norm",
    )(m, w, x, gain)


def _conv_gate(ua, ub, sa, sb, wca_ref, wcb_ref, bca_ref, bcb_ref):
    ca = sa[0] * wca_ref[0:1, :] + sa[1] * wca_ref[1:2, :] + ua * wca_ref[2:3, :] + bca_ref[...]
    cb = sb[0] * wcb_ref[0:1, :] + sb[1] * wcb_ref[1:2, :] + ub * wcb_ref[2:3, :] + bcb_ref[...]
    return jax.nn.gelu(ca, approximate=True) * cb


def _ffn_up_seq_kernel(hn_ref, wa_ref, wb_ref, wca_ref, wcb_ref, bca_ref, bcb_ref, g_ref, ta_ref, tb_ref,
                       ca_sc, cb_sc, *, tm):
    @pl.when(pl.program_id(1) == 0)
    def _():
        ca_sc[...] = jnp.zeros_like(ca_sc)
        cb_sc[...] = jnp.zeros_like(cb_sc)

    x = hn_ref[...]
    ua = _dot(x, wa_ref[...])
    ub = _dot(x, wb_ref[...])

    def shifted(u, carry_sc):
        ext = jnp.concatenate([carry_sc[...], u], axis=0)
        s1 = pltpu.roll(ext, 1, axis=0)[8:]
        s2 = pltpu.roll(ext, 2, axis=0)[8:]
        carry_sc[...] = u[tm - 8:]
        return s2, s1

    g_ref[...] = _conv_gate(ua, ub, shifted(ua, ca_sc), shifted(ub, cb_sc),
                            wca_ref, wcb_ref, bca_ref, bcb_ref).astype(g_ref.dtype)
    ta_ref[...] = ua[tm - 8:]
    tb_ref[...] = ub[tm - 8:]


def _ffn_up_seq(hn, w_up, w_conv, b_conv, *, T, ffp):
    K = hn.shape[1]
    tm = _pick(T, (512, 256, 128))
    tn = _pick(ffp, (512, 256, 128))
    nj = ffp // tn
    return pl.pallas_call(
        functools.partial(_ffn_up_seq_kernel, tm=tm),
        grid=(nj, T // tm),
        in_specs=[pl.BlockSpec((tm, K), lambda j, i: (i, 0)),
                  pl.BlockSpec((K, tn), lambda j, i: (0, j)),
                  pl.BlockSpec((K, tn), lambda j, i: (0, nj + j)),
                  pl.BlockSpec((3, tn), lambda j, i: (0, j)),
                  pl.BlockSpec((3, tn), lambda j, i: (0, nj + j)),
                  pl.BlockSpec((1, tn), lambda j, i: (0, j)),
                  pl.BlockSpec((1, tn), lambda j, i: (0, nj + j))],
        out_specs=[pl.BlockSpec((tm, tn), lambda j, i: (i, j)),
                   pl.BlockSpec((8, tn), lambda j, i: (0, j)),
                   pl.BlockSpec((8, tn), lambda j, i: (0, j))],
        out_shape=[jax.ShapeDtypeStruct((T, ffp), _MXU),
                   jax.ShapeDtypeStruct((8, ffp), _F32), jax.ShapeDtypeStruct((8, ffp), _F32)],
        scratch_shapes=[pltpu.VMEM((8, tn), _F32), pltpu.VMEM((8, tn), _F32)],
        compiler_params=_cparams(("parallel", "arbitrary")),
        name="ffn_up_conv_prompt",
    )(hn, w_up, w_up, w_conv, w_conv, b_conv, b_conv)


def _ffn_up_dec_kernel(hn_ref, wa_ref, wb_ref, wca_ref, wcb_ref, bca_ref, bcb_ref, e1a_ref, e1b_ref, e2a_ref,
                       e2b_ref, g_ref, ua_ref, ub_ref, *, Ts):
    x = hn_ref[...]
    ua = _dot(x, wa_ref[...])
    ub = _dot(x, wb_ref[...])
    r = lax.broadcasted_iota(jnp.int32, ua.shape, 0) % Ts

    def shifted(u, e1_ref, e2_ref):
        s1 = jnp.where(r == 0, e1_ref[...], pltpu.roll(u, 1, axis=0))
        s2 = jnp.where(r < 2, e2_ref[...], pltpu.roll(u, 2, axis=0))
        return s2, s1

    g_ref[...] = _conv_gate(ua, ub, shifted(ua, e1a_ref, e2a_ref), shifted(ub, e1b_ref, e2b_ref),
                            wca_ref, wcb_ref, bca_ref, bcb_ref).astype(g_ref.dtype)
    ua_ref[...] = ua
    ub_ref[...] = ub


def _ffn_up_dec(hn, w_up, w_conv, b_conv, e1, e2, *, row0, R, Ts, ffp):
    K = hn.shape[1]
    tn = _pick(ffp, (512, 256, 128))
    nj = ffp // tn
    rb0 = row0 // R
    return pl.pallas_call(
        functools.partial(_ffn_up_dec_kernel, Ts=Ts),
        grid=(nj,),
        in_specs=[pl.BlockSpec((R, K), lambda j: (rb0, 0)),
                  pl.BlockSpec((K, tn), lambda j: (0, j)),
                  pl.BlockSpec((K, tn), lambda j: (0, nj + j)),
                  pl.BlockSpec((3, tn), lambda j: (0, j)),
                  pl.BlockSpec((3, tn), lambda j: (0, nj + j)),
                  pl.BlockSpec((1, tn), lambda j: (0, j)),
                  pl.BlockSpec((1, tn), lambda j: (0, nj + j)),
                  pl.BlockSpec((R, tn), lambda j: (0, j)),
                  pl.BlockSpec((R, tn), lambda j: (0, nj + j)),
                  pl.BlockSpec((R, tn), lambda j: (0, j)),
                  pl.BlockSpec((R, tn), lambda j: (0, nj + j))],
        out_specs=[pl.BlockSpec((R, tn), lambda j: (0, j)),
                   pl.BlockSpec((R, tn), lambda j: (0, j)),
                   pl.BlockSpec((R, tn), lambda j: (0, j))],
        out_shape=[jax.ShapeDtypeStruct((R, ffp), _MXU),
                   jax.ShapeDtypeStruct((R, ffp), _F32), jax.ShapeDtypeStruct((R, ffp), _F32)],
        compiler_params=_cparams(("parallel",)),
        name="ffn_up_conv_sample",
    )(hn, w_up, w_up, w_conv, w_conv, b_conv, b_conv, e1, e1, e2, e2)


def _ffn_down_kernel(gp_ref, gs_ref, w_ref, h_ref, gn_ref, y_ref, acc_sc, *, n_k, n_p):
    i, k = pl.program_id(0), pl.program_id(1)

    @pl.when(k == 0)
    def _():
        acc_sc[...] = jnp.zeros_like(acc_sc)

    @pl.when(i < n_p)
    def _():
        acc_sc[...] += _dot(gp_ref[...], w_ref[...])

    @pl.when(i >= n_p)
    def _():
        acc_sc[...] += _dot(gs_ref[...], w_ref[...])

    @pl.when(k == n_k - 1)
    def _():
        y_ref[...] = _rms(h_ref[...] + acc_sc[...], gn_ref[...])


def _ffn_down(g_p, g_s, w, h, gain):
    T, K = g_p.shape
    R = g_s.shape[0]
    N = w.shape[1]
    tm = _pick(np.gcd(T, R), (512, 256, 128))
    tk = _pick(K, (512, 256, 128))
    n_p = T // tm
    return pl.pallas_call(
        functools.partial(_ffn_down_kernel, n_k=K // tk, n_p=n_p),
        grid=((T + R) // tm, K // tk),
        in_specs=_row_specs(tm, tk, n_p, n_col=K // tk)
                 + [pl.BlockSpec((tk, N), lambda i, k: (k, 0)),
                    pl.BlockSpec((tm, N), lambda i, k: (i, 0)),
                    pl.BlockSpec((1, N), lambda i, k: (0, 0))],
        out_specs=pl.BlockSpec((tm, N), lambda i, k: (i, 0)),
        out_shape=jax.ShapeDtypeStruct((T + R, N), _F32),
        scratch_shapes=[pltpu.VMEM((tm, N), _F32)],
        compiler_params=_cparams(("parallel", "arbitrary")),
        name="ffn_down_norm",
    )(g_p, g_s, w, h, gain)


def _pad_cols(a, n):
    return a if a.shape[-1] == n else jnp.pad(a, [(0, 0)] * (a.ndim - 1) + [(0, n - a.shape[-1])])


def kernel(x_prompt, x_sample, cache_k, cache_v, cache_idx_k, state_gla, state_ffn_conv, page_table, norm_mix, w_in, w_a2, b_a, gla_norm, w_gla_proj, w_dsa_proj, w_out, norm_ffn, w_up, w_conv, b_conv, w_down, norm_final):
    depth = w_in.shape[0]
    assert depth == 1, "single-layer trunk"
    B, T, D = x_prompt.shape
    NB, Ts, _ = x_sample.shape
    assert B == 1 and Ts == 8
    _, _, GH, dk, dv = state_gla.shape
    rank = w_a2.shape[1]
    _, n_pool, page, DH, dhd = cache_k.shape
    ihd = cache_idx_k.shape[-1]
    n_mix = w_in.shape[-1]
    IH = (n_mix - (2 * GH * dk + 2 * GH * dv + rank + 3 * DH * dhd + ihd + 2 * D)) // (ihd + 1)
    ff = w_down.shape[1]
    n_pages = page_table.shape[1]
    past = n_pages * page
    R = NB * Ts
    assert page == _LANES and ihd == _LANES and dhd == _LANES and IH + rank <= _LANES and T % R == 0

    sizes = (GH * dk, GH * dk, GH * dv, GH * dv, rank, DH * dhd, DH * dhd, DH * dhd, IH * ihd, IH, ihd, 2 * D)
    o = np.concatenate([[0], np.cumsum(sizes)])
    assert o[-1] == n_mix
    lay = dict(gla_heads=GH, gla_dk=dk, gla_dv=dv, dsa_heads=DH, dsa_hd=dhd, idx_heads=IH, idx_hd=ihd,
               iw_lane0=rank)
    segs, pos = [], 0
    for name, lo, hi in (("qg", o[0], o[1]), ("kg", o[1], o[2]), ("vg", o[2], o[3]), ("rg", o[3], o[4]),
                         ("qd", o[5], o[6]), ("kd", o[6], o[7]), ("vd", o[7], o[8]), ("iq", o[8], o[9]),
                         ("ga", o[11], o[11] + D), ("gb", o[11] + D, o[12]), ("ik", o[10], o[11])):
        lay[name] = pos
        segs.append((int(lo), int(hi)))
        pos += int(hi - lo)
    lay["small"] = pos
    nz = -(-(pos + _LANES) // 512) * 512
    w0 = w_in[0].astype(_MXU)
    w_in_p = jnp.concatenate([w0[:, lo:hi] for lo, hi in segs]
                             + [w0[:, o[4]:o[5]], w0[:, o[9]:o[10]],
                                jnp.zeros((D, nz - pos - rank - IH), _MXU)], axis=1)
    wa_p = jnp.zeros((_LANES, GH * dk), _MXU).at[:rank].set(w_a2[0].astype(_MXU))

    x_all = jnp.concatenate([x_prompt[0], x_sample.reshape(R, D)], axis=0)
    z, zb, zg = _norm_matmul(x_all, norm_mix, w_in_p, n_g_cols=lay["qd"])

    Cp = _pick(T, (128, 64, 32, 16, 8))
    a_seg = z[:, lay["small"]:lay["small"] + _LANES]
    og_p, S_p = _gla(zg, a_seg, wa_p, b_a, gla_norm, None, row0=0, n_seq=1, n_chunks=T // Cp, C=Cp, lay=lay)
    og_s, S_s = _gla(zg, a_seg, wa_p, b_a, gla_norm, state_gla[0], row0=T, n_seq=NB, n_chunks=1, C=Ts, lay=lay)

    iwT = z[:T, lay["small"] + rank:lay["small"] + rank + IH].T
    vT = zb[:T, lay["vd"]:lay["vd"] + DH * dhd].T
    maskT_p = _dsa_index(zb, iwT, T=T, topk=min(_TOPK_MAX, T // 4), lay=lay)
    od_p = _dsa_attn(zb, vT, maskT_p, T=T, lay=lay)
    od_s = _smp_attn(z, cache_idx_k[0], cache_k[0].reshape(n_pool, page * DH, dhd),
                     cache_v[0].reshape(n_pool, page * DH, dhd), page_table, row0=T, NB=NB, Ts=Ts,
                     topk=min(_TOPK_MAX, (past + Ts) // 4), lay=lay)

    mixed = _merge(og_p, og_s, od_p, od_s, w_gla_proj[0].astype(_MXU), w_dsa_proj[0].astype(_MXU), z, lay=lay)
    h, hn = _out_proj(mixed, w_out[0].astype(_MXU), x_all, norm_ffn)

    ffp = -(-ff // 512) * 512
    w_up_p = jnp.concatenate([_pad_cols(w_up[0, :, :ff].astype(_MXU), ffp),
                              _pad_cols(w_up[0, :, ff:].astype(_MXU), ffp)], axis=1)
    w_conv_p = jnp.concatenate([_pad_cols(w_conv[0, :, :ff], ffp), _pad_cols(w_conv[0, :, ff:], ffp)], axis=1)
    b_conv_p = jnp.concatenate([_pad_cols(b_conv[:, :ff], ffp), _pad_cols(b_conv[:, ff:], ffp)], axis=1)
    w_down_p = jnp.pad(w_down[0].astype(_MXU), ((0, ffp - ff), (0, 0)))
    g_p, ta, tb = _ffn_up_seq(hn, w_up_p, w_conv_p, b_conv_p, T=T, ffp=ffp)
    prev = state_ffn_conv[0]
    prev = jnp.concatenate([_pad_cols(prev[..., :ff], ffp), _pad_cols(prev[..., ff:], ffp)], axis=-1)
    e1 = jnp.pad(prev[:, 1:2], ((0, 0), (0, Ts - 1), (0, 0))).reshape(R, 2 * ffp)
    e2 = jnp.pad(prev, ((0, 0), (0, Ts - 2), (0, 0))).reshape(R, 2 * ffp)
    g_s, ua_s, ub_s = _ffn_up_dec(hn, w_up_p, w_conv_p, b_conv_p, e1, e2, row0=T, R=R, Ts=Ts, ffp=ffp)
    y = _ffn_down(g_p, g_s, w_down_p, h, norm_final.reshape(1, D))

    def seg(name, width, rows):
        return z[rows, lay[name]:lay[name] + width]

    rp, rs = slice(0, T), slice(T, T + R)
    conv_p = jnp.concatenate([ta[6:8, :ff], tb[6:8, :ff]], axis=-1)
    conv_s = jnp.concatenate([ua_s.reshape(NB, Ts, ffp)[:, Ts - 2:, :ff], ub_s.reshape(NB, Ts, ffp)[:, Ts - 2:, :ff]],
                             axis=-1)
    return (y[rp].reshape(1, T, D), y[rs].reshape(NB, Ts, D),
            seg("kd", DH * dhd, rp).reshape(1, 1, T, DH, dhd), seg("vd", DH * dhd, rp).reshape(1, 1, T, DH, dhd),
            seg("ik", ihd, rp).reshape(1, 1, T, ihd),
            S_p.reshape(1, 1, GH, dk, dv), conv_p.reshape(1, 1, 2, 2 * ff),
            seg("kd", DH * dhd, rs).reshape(1, NB, Ts, DH, dhd), seg("vd", DH * dhd, rs).reshape(1, NB, Ts, DH, dhd),
            seg("ik", ihd, rs).reshape(1, NB, Ts, ihd),
            S_s.reshape(1, NB, GH, dk, dv), conv_s.reshape(1, NB, 2, 2 * ff))
```
